```python
import jax, jax.numpy as jnp
from jax import lax
import numpy as np

D_MODEL = 1024
BATCH = 8
SEQ = 4096
DEPTH = 1
DEC_BATCH = 128
DEC_SEQ = 8
PAST_LEN = 8192
PAGE_SIZE = 128

MIX_WIDTH = D_MODEL
MOBA_WIDTH = MIX_WIDTH // 2
MOBA_HEAD_DIM = 128
MOBA_HEADS = MOBA_WIDTH // MOBA_HEAD_DIM
MOBA_BLOCK = 256
MOBA_TOPK = 3
Q_BLOCK = 128
HGRN_WIDTH = MIX_WIDTH - MOBA_WIDTH
HGRN_EXPAND = 128
HGRN_HEADS = HGRN_WIDTH // HGRN_EXPAND
HGRN_DV = HGRN_WIDTH // HGRN_HEADS
HGRN_CHUNK = 64
MEM_LEN = 256
XA_HEADS = 4
XA_HEAD_DIM = D_MODEL // XA_HEADS
D_FF = 4 * D_MODEL
ROPE_THETA = 10000.0
NORM_EPS = 1e-6
PROJ_SIZES = (MOBA_WIDTH, MOBA_WIDTH, MOBA_WIDTH, HGRN_WIDTH, HGRN_WIDTH, HGRN_WIDTH, HGRN_WIDTH)
PROJ_DIM = sum(PROJ_SIZES)

kernel_name = 'hymba_moba_hgrn2_decoder_step'


def rms_norm(x, gain):
    xf = x.astype(jnp.float32)
    y = xf * lax.rsqrt(jnp.mean(xf * xf, axis=-1, keepdims=True) + NORM_EPS)
    return (y * gain.astype(jnp.float32)).astype(x.dtype)


def rotary(x, pos):
    half = x.shape[-1] // 2
    inv_freq = ROPE_THETA ** (-jnp.arange(half, dtype=jnp.float32) / half)
    ang = pos.astype(jnp.float32)[:, None] * inv_freq[None, :]
    cos = jnp.cos(ang)[:, None, :]
    sin = jnp.sin(ang)[:, None, :]
    xf = x.astype(jnp.float32)
    x1, x2 = xf[..., :half], xf[..., half:]
    return jnp.concatenate([x1 * cos - x2 * sin, x2 * cos + x1 * sin], -1).astype(x.dtype)


def project_mixers(hn, w_in, lb, pos):
    B, L, _ = hn.shape
    u = hn @ w_in
    cuts = [int(c) for c in np.cumsum(PROJ_SIZES)[:-1]]
    mq, mk, mv, hq, hf, hi, hg = jnp.split(u, cuts, axis=-1)
    shp = (B, L, MOBA_HEADS, MOBA_HEAD_DIM)
    mq = rotary(mq.reshape(shp), pos)
    mk = rotary(mk.reshape(shp), pos)
    mv = mv.reshape(shp)

    def heads(a, d):
        return a.reshape(B, L, HGRN_HEADS, d).transpose(0, 2, 1, 3)

    f = lb + (1.0 - lb) * jax.nn.sigmoid(hf.astype(jnp.float32))
    hq = heads(jax.nn.silu(hq), HGRN_EXPAND)
    hk = heads(1.0 - f, HGRN_EXPAND)
    hlogf = heads(jnp.log(f), HGRN_EXPAND)
    hv = heads(hi, HGRN_DV)
    return (mq, mk, mv), (hq, hk, hv, hlogf), hg


def hgrn2_chunked(q, k, v, logf, s0):
    B, H, L, DK = q.shape
    DV = v.shape[-1]
    c = min(HGRN_CHUNK, L)
    n = L // c

    def chunks(a):
        return a.astype(jnp.float32).reshape(B, H, n, c, a.shape[-1]).transpose(2, 0, 1, 3, 4)

    causal = jnp.tril(jnp.ones((c, c), dtype=bool))[None, None, :, :, None]

    def step(S, xs):
        qc, kc, vc, gc = xs
        G = jnp.cumsum(gc, axis=2)
        o = jnp.einsum('bhtd,bhde->bhte', qc * jnp.exp(G), S)
        decay = jnp.exp(jnp.where(causal, G[:, :, :, None, :] - G[:, :, None, :, :], -jnp.inf))
        att = jnp.einsum('bhtd,bhsd,bhtsd->bhts', qc, kc, decay)
        o = o + jnp.einsum('bhts,bhse->bhte', att, vc)
        g_end = G[:, :, -1:, :]
        S = jnp.exp(g_end[:, :, 0, :, None]) * S + jnp.einsum('bhsd,bhse->bhde', kc * jnp.exp(g_end - G), vc)
        return S, o

    S, o = lax.scan(step, s0.astype(jnp.float32), (chunks(q), chunks(k), chunks(v), chunks(logf)))
    o = o.transpose(1, 2, 0, 3, 4).reshape(B, H, L, DV)
    return o.astype(v.dtype), S.astype(s0.dtype)


def moba_prompt(q, k, v):
    B, S, H, D = q.shape
    nb = -(-S // MOBA_BLOCK)
    nf = S // MOBA_BLOCK
    k_top = min(MOBA_TOPK, nb - 1)
    nqb = S // Q_BLOCK
    scale = D ** -0.5
    pad = nb * MOBA_BLOCK - S

    def blocks(a):
        a = jnp.pad(a, ((0, 0), (0, pad), (0, 0), (0, 0)))
        return a.reshape(B, nb, MOBA_BLOCK, H, D).transpose(0, 3, 1, 2, 4)

    kb, vb = blocks(k), blocks(v)

    def per_qblock(a):
        a = a.reshape((B, H, nqb, Q_BLOCK) + a.shape[3:])
        return jnp.swapaxes(a, 1, 2).reshape((B * nqb, H, Q_BLOCK) + a.shape[4:])

    xs = [per_qblock(q.transpose(0, 2, 1, 3)),
          jnp.repeat(jnp.arange(B), nqb), jnp.tile(jnp.arange(nqb), B)]
    if k_top > 0:
        kmean = jnp.mean(kb[:, :, :nf].astype(jnp.float32), axis=3)
        gate = jnp.einsum('bshd,bhnd->bhsn', q.astype(jnp.float32), kmean)
        past = jnp.arange(nf)[None, :] < (jnp.arange(S) // MOBA_BLOCK)[:, None]
        gate = jnp.where(past, gate, -jnp.inf)
        top_val, top_idx = lax.top_k(gate, k_top)
        xs += [per_qblock(top_idx), per_qblock(jnp.isfinite(top_val))]
    h_ix = jnp.arange(H)[:, None, None]
    n_sel = k_top * MOBA_BLOCK

    def attend(xs_i):
        qc, bi, ci = xs_i[0], xs_i[1], xs_i[2]
        qpos = ci * Q_BLOCK + jnp.arange(Q_BLOCK)
        ob = qpos[0] // MOBA_BLOCK
        kpos = ob * MOBA_BLOCK + jnp.arange(MOBA_BLOCK)
        k_own = kb[bi][:, ob]
        v_own = vb[bi][:, ob]
        s_own = jnp.einsum('hqd,hkd->hqk', qc, k_own).astype(jnp.float32) * scale
        s_own = jnp.where(kpos[None, None, :] <= qpos[None, :, None], s_own, -jnp.inf)
        if k_top == 0:
            p = jax.nn.softmax(s_own, axis=-1).astype(v.dtype)
            return jnp.einsum('hqk,hkd->hqd', p, v_own)
        idx, ok = xs_i[3], xs_i[4]
        k_sel = kb[bi, h_ix, idx]
        v_sel = vb[bi, h_ix, idx]
        s_sel = jnp.einsum('hqd,hqnkd->hqnk', qc, k_sel).astype(jnp.float32) * scale
        s_sel = jnp.where(ok[..., None], s_sel, -jnp.inf)
        s_all = jnp.concatenate([s_sel.reshape(H, Q_BLOCK, n_sel), s_own], -1)
        p = jax.nn.softmax(s_all, axis=-1).astype(v.dtype)
        p_sel = p[..., :n_sel].reshape(H, Q_BLOCK, k_top, MOBA_BLOCK)
        return (jnp.einsum('hqnk,hqnkd->hqd', p_sel, v_sel)
                + jnp.einsum('hqk,hkd->hqd', p[..., n_sel:], v_own))

    out = lax.map(attend, tuple(xs))
    return out.reshape(B, nqb, H, Q_BLOCK, D).transpose(0, 1, 3, 2, 4).reshape(B, S, H * D)


def moba_sample(q, k, v, cache_k, cache_v, page_table, layer):
    DB, T, H, D = q.shape
    scale = D ** -0.5
    ppb = MOBA_BLOCK // PAGE_SIZE
    n_full = PAST_LEN // MOBA_BLOCK
    k_top = min(MOBA_TOPK, n_full)
    own_page0 = n_full * ppb
    n_pages = page_table.shape[1]
    r = (n_pages - own_page0) * PAGE_SIZE
    own_mask = jnp.arange(r + T)[None, :] <= (r + jnp.arange(T))[:, None]
    h_ix = jnp.arange(H)[:, None, None, None]
    n_sel = k_top * MOBA_BLOCK

    def one_seq(xs):
        qs, ks, vs, pt = xs
        if r > 0:
            k_own = jnp.concatenate([cache_k[layer, pt[own_page0:]].reshape(r, H, D), ks], 0)
            v_own = jnp.concatenate([cache_v[layer, pt[own_page0:]].reshape(r, H, D), vs], 0)
        else:
            k_own, v_own = ks, vs
        s_own = jnp.einsum('thd,nhd->htn', qs, k_own).astype(jnp.float32) * scale
        s_own = jnp.where(own_mask[None], s_own, -jnp.inf)
        if k_top == 0:
            p = jax.nn.softmax(s_own, axis=-1).astype(vs.dtype)
            return jnp.einsum('htn,nhd->thd', p, v_own)
        k_past = cache_k[layer, pt[:own_page0]].astype(jnp.float32)
        kmean = jnp.mean(k_past.reshape(n_full, MOBA_BLOCK, H, D), axis=1)
        gate = jnp.einsum('thd,nhd->htn', qs.astype(jnp.float32), kmean)
        _, idx = lax.top_k(gate, k_top)
        phys = pt[idx[..., None] * ppb + jnp.arange(ppb)]
        k_sel = cache_k[layer, phys, :, h_ix, :].reshape(H, T, n_sel, D)
        v_sel = cache_v[layer, phys, :, h_ix, :].reshape(H, T, n_sel, D)
        s_sel = jnp.einsum('thd,htnd->htn', qs, k_sel).astype(jnp.float32) * scale
        p = jax.nn.softmax(jnp.concatenate([s_sel, s_own], -1), axis=-1).astype(vs.dtype)
        return (jnp.einsum('htn,htnd->thd', p[..., :n_sel], v_sel)
                + jnp.einsum('htn,nhd->thd', p[..., n_sel:], v_own))

    out = lax.map(one_seq, (q, k, v, page_table))
    return out.reshape(DB, T, H * D)


def token_mixing(att, ho, hg, hgrn_gain, w_out):
    B, H, L, DV = ho.shape
    hout = rms_norm(ho.transpose(0, 2, 1, 3), hgrn_gain).reshape(B, L, H * DV) * jax.nn.silu(hg)
    return jnp.concatenate([att, hout], -1) @ w_out


def memory_kv(mem, gain, w_ck, w_cv):
    B, M, _ = mem.shape
    mn = rms_norm(mem, gain)
    shp = (B, M, XA_HEADS, XA_HEAD_DIM)
    return (mn @ w_ck).reshape(shp), (mn @ w_cv).reshape(shp)


def cross_attend(hn, mk, mv, w_cq, w_co):
    B, L, _ = hn.shape
    q = (hn @ w_cq).reshape(B, L, XA_HEADS, XA_HEAD_DIM)
    s = jnp.einsum('blhd,bmhd->bhlm', q, mk).astype(jnp.float32) * (XA_HEAD_DIM ** -0.5)
    p = jax.nn.softmax(s, axis=-1).astype(mv.dtype)
    o = jnp.einsum('bhlm,bmhd->blhd', p, mv).reshape(B, L, XA_HEADS * XA_HEAD_DIM)
    return o @ w_co


def sq_relu_mlp(hn, w_up, w_down):
    return jnp.square(jax.nn.relu(hn @ w_up)) @ w_down


def setup_inputs(seed: int = 0) -> dict:
    key = jax.random.key(seed)
    ks = jax.random.split(key, 32)
    f32 = jnp.float32
    n_pages = PAST_LEN // PAGE_SIZE
    n_pool = (DEC_BATCH * n_pages * 5 + 3) // 4

    def nrm(k, shape, scale=1.0):
        return scale * jax.random.normal(k, shape, f32)

    def gain(k, shape):
        return 1.0 + 0.05 * jax.random.normal(k, shape, f32)

    page_table = jax.random.permutation(ks[7], n_pool)[:DEC_BATCH * n_pages]
    page_table = page_table.reshape(DEC_BATCH, n_pages).astype(jnp.int32)
    kv_shape = (DEPTH, n_pool, PAGE_SIZE, MOBA_HEADS, MOBA_HEAD_DIM)
    mem_shape = (DEPTH, DEC_BATCH, MEM_LEN, XA_HEADS, XA_HEAD_DIM)
    return {
        'x_prompt': nrm(ks[0], (BATCH, SEQ, D_MODEL)),
        'x_sample': nrm(ks[1], (DEC_BATCH, DEC_SEQ, D_MODEL)),
        'cache_k': nrm(ks[2], kv_shape),
        'cache_v': nrm(ks[3], kv_shape),
        'state_hgrn': nrm(ks[4], (DEPTH, DEC_BATCH, HGRN_HEADS, HGRN_EXPAND, HGRN_DV), 0.5),
        'cache_mem_k': nrm(ks[5], mem_shape),
        'cache_mem_v': nrm(ks[6], mem_shape),
        'page_table': page_table,
        'mem_prompt': nrm(ks[8], (BATCH, MEM_LEN, D_MODEL)),
        'norm_pre_mix': gain(ks[9], (DEPTH, D_MODEL)),
        'w_in': nrm(ks[10], (DEPTH, D_MODEL, PROJ_DIM), D_MODEL ** -0.5),
        'hgrn_lb': nrm(ks[11], (DEPTH + 1, HGRN_WIDTH), 0.1),
        'hgrn_norm': gain(ks[12], (DEPTH, HGRN_DV)),
        'w_out': nrm(ks[13], (DEPTH, MIX_WIDTH, D_MODEL), MIX_WIDTH ** -0.5),
        'norm_post_mix': gain(ks[14], (DEPTH, D_MODEL)),
        'norm_mem': gain(ks[15], (DEPTH, D_MODEL)),
        'norm_pre_x': gain(ks[16], (DEPTH, D_MODEL)),
        'w_cq': nrm(ks[17], (DEPTH, D_MODEL, XA_HEADS * XA_HEAD_DIM), D_MODEL ** -0.5),
        'w_ck': nrm(ks[18], (DEPTH, D_MODEL, XA_HEADS * XA_HEAD_DIM), D_MODEL ** -0.5),
        'w_cv': nrm(ks[19], (DEPTH, D_MODEL, XA_HEADS * XA_HEAD_DIM), D_MODEL ** -0.5),
        'w_co': nrm(ks[20], (DEPTH, XA_HEADS * XA_HEAD_DIM, D_MODEL), (XA_HEADS * XA_HEAD_DIM) ** -0.5),
        'norm_post_x': gain(ks[21], (DEPTH, D_MODEL)),
        'norm_pre_ffn': gain(ks[22], (DEPTH, D_MODEL)),
        'w_up': nrm(ks[23], (DEPTH, D_MODEL, D_FF), D_MODEL ** -0.5),
        'w_down': nrm(ks[24], (DEPTH, D_FF, D_MODEL), D_FF ** -0.5),
        'norm_post_ffn': gain(ks[25], (DEPTH, D_MODEL)),
    }


def reference(x_prompt, x_sample, cache_k, cache_v, state_hgrn, cache_mem_k, cache_mem_v,
              page_table, mem_prompt, norm_pre_mix, w_in, hgrn_lb, hgrn_norm, w_out,
              norm_post_mix, norm_mem, norm_pre_x, w_cq, w_ck, w_cv, w_co, norm_post_x,
              norm_pre_ffn, w_up, w_down, norm_post_ffn):
    pos_p = jnp.arange(x_prompt.shape[1])
    pos_s = PAST_LEN + jnp.arange(x_sample.shape[1])
    lb_all = jnp.cumsum(jax.nn.softmax(hgrn_lb.astype(jnp.float32), axis=0), axis=0)
    yp, ys = x_prompt, x_sample
    kp_l, vp_l, sp_l, mkp_l, mvp_l, ks_l, vs_l, ss_l = [], [], [], [], [], [], [], []
    for l in range(DEPTH):
        hn = rms_norm(yp, norm_pre_mix[l])
        (mq, mk, mv), (hq, hk, hv, hlf), hg = project_mixers(hn, w_in[l], lb_all[l], pos_p)
        att = moba_prompt(mq, mk, mv)
        s0 = jnp.zeros((yp.shape[0], HGRN_HEADS, HGRN_EXPAND, HGRN_DV), state_hgrn.dtype)
        ho, s_fin = hgrn2_chunked(hq, hk, hv, hlf, s0)
        yp = yp + rms_norm(token_mixing(att, ho, hg, hgrn_norm[l], w_out[l]), norm_post_mix[l])
        mkp, mvp = memory_kv(mem_prompt, norm_mem[l], w_ck[l], w_cv[l])
        yp = yp + rms_norm(cross_attend(rms_norm(yp, norm_pre_x[l]), mkp, mvp, w_cq[l], w_co[l]), norm_post_x[l])
        yp = yp + rms_norm(sq_relu_mlp(rms_norm(yp, norm_pre_ffn[l]), w_up[l], w_down[l]), norm_post_ffn[l])
        kp_l.append(mk); vp_l.append(mv); sp_l.append(s_fin); mkp_l.append(mkp); mvp_l.append(mvp)
        hn = rms_norm(ys, norm_pre_mix[l])
        (mq, mk, mv), (hq, hk, hv, hlf), hg = project_mixers(hn, w_in[l], lb_all[l], pos_s)
        att = moba_sample(mq, mk, mv, cache_k, cache_v, page_table, l)
        ho, s_new = hgrn2_chunked(hq, hk, hv, hlf, state_hgrn[l])
        ys = ys + rms_norm(token_mixing(att, ho, hg, hgrn_norm[l], w_out[l]), norm_post_mix[l])
        ys = ys + rms_norm(cross_attend(rms_norm(ys, norm_pre_x[l]), cache_mem_k[l], cache_mem_v[l],
                                        w_cq[l], w_co[l]), norm_post_x[l])
        ys = ys + rms_norm(sq_relu_mlp(rms_norm(ys, norm_pre_ffn[l]), w_up[l], w_down[l]), norm_post_ffn[l])
        ks_l.append(mk); vs_l.append(mv); ss_l.append(s_new)
    k_prompt = jnp.stack(kp_l)
    v_prompt = jnp.stack(vp_l)
    hgrn_prompt = jnp.stack(sp_l)
    mem_k_prompt = jnp.stack(mkp_l)
    mem_v_prompt = jnp.stack(mvp_l)
    k_sample = jnp.stack(ks_l)
    v_sample = jnp.stack(vs_l)
    hgrn_sample = jnp.stack(ss_l)
    return (yp, ys, k_prompt, v_prompt, hgrn_prompt, mem_k_prompt, mem_v_prompt, k_sample, v_sample, hgrn_sample)
```

```python
import functools

import jax
import jax.numpy as jnp
from jax import lax
from jax.experimental import pallas as pl
from jax.experimental.pallas import tpu as pltpu

F32 = jnp.float32
BF16 = jnp.bfloat16

NORM_EPS = 1e-6
ROPE_THETA = 10000.0
HEAD_DIM = 128
N_HEADS = 4
MIX_HALF = N_HEADS * HEAD_DIM
MOBA_BLOCK = 256
MOBA_TOPK = 3
PAGE_SIZE = 128
HGRN_CHUNK = 64
HGRN_SUB = 16
XA_HEADS = 4
V7X_VMEM_LIMIT = 56 * 1024 * 1024


def _bf(x):
    return x.astype(BF16)


def _mm(a, b):
    return jnp.dot(_bf(a), _bf(b), preferred_element_type=F32)


def _mm_nt(a, b):
    return lax.dot_general(_bf(a), _bf(b), (((1,), (1,)), ((), ())), preferred_element_type=F32)


def _mm_tn(a, b):
    return lax.dot_general(_bf(a), _bf(b), (((0,), (0,)), ((), ())), preferred_element_type=F32)


def _split2(x):
    hi = x.astype(BF16)
    lo = (x - hi.astype(F32)).astype(BF16)
    return hi, lo


def _split3(x):
    a = x.astype(BF16)
    r = x - a.astype(F32)
    b = r.astype(BF16)
    c = (r - b.astype(F32)).astype(BF16)
    return a, b, c


def _mm3_nt(a, b):
    a_hi, a_lo = _split2(a)
    b_hi, b_lo = _split2(b)
    dn = (((1,), (1,)), ((), ()))
    dot = functools.partial(lax.dot_general, dimension_numbers=dn, preferred_element_type=F32)
    return (dot(a_lo, b_hi) + dot(a_hi, b_lo)) + dot(a_hi, b_hi)


def _rms(x, gain):
    return x * lax.rsqrt(jnp.mean(x * x, axis=-1, keepdims=True) + NORM_EPS) * gain


def _silu(x):
    return x * (1.0 / (1.0 + jnp.exp(-x)))


def _sigmoid(x):
    return 1.0 / (1.0 + jnp.exp(-x))


def _cumsum_rows(g, lmat):
    g1, g2, g3 = _split3(g)
    dot = functools.partial(jnp.dot, preferred_element_type=F32)
    return (dot(lmat, g3) + dot(lmat, g2)) + dot(lmat, g1)


def _chunk_matrices(rows, chunk, sub):
    r = lax.broadcasted_iota(jnp.int32, (rows, rows), 0)
    c = lax.broadcasted_iota(jnp.int32, (rows, rows), 1)
    same_chunk = (r // chunk) == (c // chunk)
    incl = same_chunk & (c <= r)
    start = same_chunk & (c < (r // sub) * sub)
    end = same_chunk & (c < (r // sub + 1) * sub)
    as_bf = lambda m: jnp.where(m, 1.0, 0.0).astype(BF16)
    return as_bf(incl), as_bf(start), as_bf(end)


def _hgrn_chunk(q, k, v, G, Bs, Be, S, sub):
    c = q.shape[0]
    nsub = c // sub
    qh = q * jnp.exp(G - Bs)
    kd = k * jnp.exp(Bs - G)
    qg = q * jnp.exp(G)
    row = lax.broadcasted_iota(jnp.int32, (c, c), 0)
    col = lax.broadcasted_iota(jnp.int32, (c, c), 1)
    if nsub == 1:
        att = jnp.where(col <= row, _mm_nt(qh, kd), 0.0)
        o = _mm(att, v) + _mm(qg, S)
    else:
        kh = k * jnp.exp(Be - G)
        lhs = [qh]
        z = jnp.zeros((sub, HEAD_DIM), F32)
        rhs_rows = [jnp.concatenate([kd] + [jnp.zeros_like(kd)] * (nsub - 1), axis=1)]
        for j in range(nsub - 1):
            g_end_j = G[(j + 1) * sub - 1:(j + 1) * sub, :]
            lhs.append(q * jnp.exp(jnp.minimum(G - g_end_j, 0.0)))
            parts = [z] * nsub
            parts[j + 1] = kh[j * sub:(j + 1) * sub, :]
            rhs_rows.append(jnp.concatenate(parts, axis=1))
        n_off = (nsub - 1) * sub
        pad = (-(c + n_off)) % HEAD_DIM
        if pad:
            rhs_rows.append(jnp.zeros((pad, nsub * HEAD_DIM), F32))
        R = _mm_nt(jnp.concatenate(lhs, axis=1), jnp.concatenate(rhs_rows, axis=0))
        width = c + n_off + pad
        row2 = lax.broadcasted_iota(jnp.int32, (c, width), 0)
        col2 = lax.broadcasted_iota(jnp.int32, (c, width), 1)
        diag_ok = (col2 < c) & (col2 <= row2) & ((col2 // sub) == (row2 // sub))
        off_ok = (col2 >= c) & (col2 < c + n_off) & (((col2 - c) // sub) < (row2 // sub))
        att2 = jnp.where(diag_ok | off_ok, R, 0.0)
        v_off = v[:n_off, :]
        v_rows = [v, v_off]
        if pad:
            v_rows.append(jnp.zeros((pad, HEAD_DIM), F32))
        o = _mm(jnp.concatenate([att2, qg], axis=1), jnp.concatenate(v_rows + [S], axis=0))
    g_end = G[c - 1:c, :]
    kk = k * jnp.exp(g_end - G)
    decay = jnp.broadcast_to(jnp.exp(g_end), (HEAD_DIM, HEAD_DIM)).T
    S_new = decay * S + _mm_tn(kk, v)
    return o, S_new


def _rotary(x, cos, sin):
    return x * cos + pltpu.roll(x, HEAD_DIM // 2, 1) * sin


def _project(x, gain, wqk_hi, wqk_lo, w_rest, cos, sin, lb):
    hn = _rms(x, gain)
    hn_hi, hn_lo = _split2(hn)
    dot = functools.partial(jnp.dot, preferred_element_type=F32)
    qk = (dot(hn_lo, wqk_hi) + dot(hn_hi, wqk_lo)) + dot(hn_hi, wqk_hi)
    rest = dot(hn_hi, w_rest)
    q = jnp.concatenate([_rotary(qk[:, h * HEAD_DIM:(h + 1) * HEAD_DIM], cos, sin) for h in range(N_HEADS)], axis=1)
    k = jnp.concatenate([_rotary(qk[:, MIX_HALF + h * HEAD_DIM:MIX_HALF + (h + 1) * HEAD_DIM], cos, sin)
                         for h in range(N_HEADS)], axis=1)
    v = rest[:, 0:MIX_HALF]
    hq = _silu(rest[:, MIX_HALF:2 * MIX_HALF])
    f = lb + (1.0 - lb) * _sigmoid(rest[:, 2 * MIX_HALF:3 * MIX_HALF])
    hk = 1.0 - f
    g = jnp.log(f)
    hv = rest[:, 3 * MIX_HALF:4 * MIX_HALF]
    hg = rest[:, 4 * MIX_HALF:5 * MIX_HALF]
    return q, k, v, hq, hk, hv, g, hg


def _hgrn_out(o, gain, hg):
    return _rms(o, gain) * _silu(hg)


def _proj_prompt_kernel(x_ref, gain_ref, wqk_hi_ref, wqk_lo_ref, wrest_ref, cos_ref, sin_ref, lb_ref, hgain_ref,
                        q_ref, k_ref, v_ref, kbf_ref, vt_ref, kmean_ref, hout_ref, state_ref):
    t = pl.program_id(1)

    @pl.when(t == 0)
    def _():
        state_ref[...] = jnp.zeros_like(state_ref)

    ts = x_ref.shape[1]
    q, k, v, hq, hk, hv, g, hg = _project(x_ref[0], gain_ref[...], wqk_hi_ref[...], wqk_lo_ref[...], wrest_ref[...],
                                          cos_ref[...], sin_ref[...], lb_ref[...])
    q_ref[0] = q
    k_ref[0] = k
    v_ref[0] = v
    kbf_ref[0] = _bf(k)
    n_blk = ts // MOBA_BLOCK
    for n in range(n_blk):
        sl = slice(n * MOBA_BLOCK, (n + 1) * MOBA_BLOCK)
        vt_ref[0, n] = _bf(v[sl, :].T)
        kmean_ref[0, pl.ds(t * n_blk + n, 1), :] = jnp.mean(k[sl, :], axis=0, keepdims=True)

    l_incl, l_start, l_end = _chunk_matrices(ts, HGRN_CHUNK, HGRN_SUB)
    G = _cumsum_rows(g, l_incl)
    Bs = _cumsum_rows(g, l_start)
    Be = _cumsum_rows(g, l_end)
    hgain = hgain_ref[...]
    for h in range(N_HEADS):
        hs = slice(h * HEAD_DIM, (h + 1) * HEAD_DIM)
        S = state_ref[0, h]
        outs = []
        for ci in range(ts // HGRN_CHUNK):
            rs = slice(ci * HGRN_CHUNK, (ci + 1) * HGRN_CHUNK)
            o, S = _hgrn_chunk(hq[rs, hs], hk[rs, hs], hv[rs, hs], G[rs, hs], Bs[rs, hs], Be[rs, hs], S, HGRN_SUB)
            outs.append(o)
        state_ref[0, h] = S
        hout_ref[0, :, hs] = _bf(_hgrn_out(jnp.concatenate(outs, axis=0), hgain, hg[:, hs]))


def _proj_sample_kernel(x_ref, gain_ref, wqk_hi_ref, wqk_lo_ref, wrest_ref, cos_ref, sin_ref, lb_ref, hgain_ref,
                        s0_ref, q_ref, k_ref, v_ref, hout_ref, state_ref, *, seq_len):
    rows = x_ref.shape[0]
    q, k, v, hq, hk, hv, g, hg = _project(x_ref[...], gain_ref[...], wqk_hi_ref[...], wqk_lo_ref[...], wrest_ref[...],
                                          cos_ref[...], sin_ref[...], lb_ref[...])
    q_ref[...] = q
    k_ref[...] = k
    v_ref[...] = v
    l_incl, _, _ = _chunk_matrices(rows, seq_len, seq_len)
    G = _cumsum_rows(g, l_incl)
    zero = jnp.zeros((seq_len, HEAD_DIM), F32)
    hgain = hgain_ref[...]
    for h in range(N_HEADS):
        hs = slice(h * HEAD_DIM, (h + 1) * HEAD_DIM)
        outs = []
        for si in range(rows // seq_len):
            rs = slice(si * seq_len, (si + 1) * seq_len)
            o, S = _hgrn_chunk(hq[rs, hs], hk[rs, hs], hv[rs, hs], G[rs, hs], zero, zero, s0_ref[si, h], seq_len)
            state_ref[si, h] = S
            outs.append(o)
        hout_ref[:, hs] = _bf(_hgrn_out(jnp.concatenate(outs, axis=0), hgain, hg[:, hs]))


def _rotary_tables(pos):
    half = HEAD_DIM // 2
    inv_freq = ROPE_THETA ** (-jnp.arange(half, dtype=F32) / half)
    ang = pos.astype(F32)[:, None] * inv_freq[None, :]
    cos = jnp.cos(ang)
    sin = jnp.sin(ang)
    return jnp.concatenate([cos, cos], axis=1), jnp.concatenate([-sin, sin], axis=1)


def _full(shape):
    return pl.BlockSpec(shape, lambda *_: (0,) * len(shape))


def _proj_prompt(x, gain, wqk_hi, wqk_lo, w_rest, lb, hgain, ts):
    B, S, D = x.shape
    nb = S // MOBA_BLOCK
    cos, sin = _rotary_tables(jnp.arange(S))
    out_shape = (
        jax.ShapeDtypeStruct((B, S, MIX_HALF), F32),
        jax.ShapeDtypeStruct((B, S, MIX_HALF), F32),
        jax.ShapeDtypeStruct((B, S, MIX_HALF), F32),
        jax.ShapeDtypeStruct((B, S, MIX_HALF), BF16),
        jax.ShapeDtypeStruct((B, nb, MIX_HALF, MOBA_BLOCK), BF16),
        jax.ShapeDtypeStruct((B, nb, MIX_HALF), F32),
        jax.ShapeDtypeStruct((B, S, MIX_HALF), BF16),
        jax.ShapeDtypeStruct((B, N_HEADS, HEAD_DIM, HEAD_DIM), F32),
    )
    tok = pl.BlockSpec((1, ts, MIX_HALF), lambda b, t: (b, t, 0))
    return pl.pallas_call(
        _proj_prompt_kernel,
        grid=(B, S // ts),
        in_specs=[
            pl.BlockSpec((1, ts, D), lambda b, t: (b, t, 0)),
            _full((1, D)), _full(wqk_hi.shape), _full(wqk_lo.shape), _full(w_rest.shape),
            pl.BlockSpec((ts, HEAD_DIM), lambda b, t: (t, 0)),
            pl.BlockSpec((ts, HEAD_DIM), lambda b, t: (t, 0)),
            _full((1, MIX_HALF)), _full((1, HEAD_DIM)),
        ],
        out_specs=(
            tok, tok, tok, tok,
            pl.BlockSpec((1, ts // MOBA_BLOCK, MIX_HALF, MOBA_BLOCK), lambda b, t: (b, t, 0, 0)),
            pl.BlockSpec((1, nb, MIX_HALF), lambda b, t: (b, 0, 0)),
            tok,
            pl.BlockSpec((1, N_HEADS, HEAD_DIM, HEAD_DIM), lambda b, t: (b, 0, 0, 0)),
        ),
        out_shape=out_shape,
        compiler_params=pltpu.CompilerParams(dimension_semantics=("arbitrary", "arbitrary"),
                                             vmem_limit_bytes=V7X_VMEM_LIMIT),
        name="proj_prompt",
    )(x, gain, wqk_hi, wqk_lo, w_rest, cos, sin, lb, hgain)


def _proj_sample(x, gain, wqk_hi, wqk_lo, w_rest, lb, hgain, state0, past_len, seqs_per_tile):
    DB, T, D = x.shape
    rows = seqs_per_tile * T
    cos, sin = _rotary_tables(past_len + jnp.arange(T))
    cos = jnp.tile(cos, (seqs_per_tile, 1))
    sin = jnp.tile(sin, (seqs_per_tile, 1))
    out_shape = (
        jax.ShapeDtypeStruct((DB * T, MIX_HALF), F32),
        jax.ShapeDtypeStruct((DB * T, MIX_HALF), F32),
        jax.ShapeDtypeStruct((DB * T, MIX_HALF), F32),
        jax.ShapeDtypeStruct((DB * T, MIX_HALF), BF16),
        jax.ShapeDtypeStruct((DB, N_HEADS, HEAD_DIM, HEAD_DIM), F32),
    )
    tok = pl.BlockSpec((rows, MIX_HALF), lambda i: (i, 0))
    st = pl.BlockSpec((seqs_per_tile, N_HEADS, HEAD_DIM, HEAD_DIM), lambda i: (i, 0, 0, 0))
    return pl.pallas_call(
        functools.partial(_proj_sample_kernel, seq_len=T),
        grid=(DB // seqs_per_tile,),
        in_specs=[
            pl.BlockSpec((rows, D), lambda i: (i, 0)),
            _full((1, D)), _full(wqk_hi.shape), _full(wqk_lo.shape), _full(w_rest.shape),
            _full((rows, HEAD_DIM)), _full((rows, HEAD_DIM)),
            _full((1, MIX_HALF)), _full((1, HEAD_DIM)),
            st,
        ],
        out_specs=(tok, tok, tok, tok, st),
        out_shape=out_shape,
        compiler_params=pltpu.CompilerParams(dimension_semantics=("arbitrary",),
                                             vmem_limit_bytes=V7X_VMEM_LIMIT),
        name="proj_sample",
    )(x.reshape(DB * T, D), gain, wqk_hi, wqk_lo, w_rest, cos, sin, lb, hgain, state0)


def _top_blocks(gate_t, n_valid, k_top):
    nb = gate_t.shape[0]
    blk = lax.broadcasted_iota(jnp.int32, gate_t.shape, 0)
    neg = jnp.float32(-jnp.inf)
    g = jnp.where(blk < n_valid, gate_t, neg)
    sel = jnp.zeros(gate_t.shape, F32)
    for _ in range(k_top):
        m = jnp.max(g, axis=0, keepdims=True)
        first = jnp.min(jnp.where(g == m, blk, nb), axis=0, keepdims=True)
        pick = blk == first
        sel = jnp.where(pick & (m > neg), 1.0, sel)
        g = jnp.where(pick, neg, g)
    return sel


def _moba_prompt_kernel(q_ref, kbf_ref, vt_ref, kmean_ref, o_ref, sel_ref, *, k_top):
    i = pl.program_id(2)
    q = q_ref[0]
    sel_ref[...] = _top_blocks(_mm3_nt(kmean_ref[0], q), i, k_top)
    qs = _bf(q * (HEAD_DIM ** -0.5))
    neg = jnp.float32(-jnp.inf)

    start = pl.multiple_of(i * MOBA_BLOCK, MOBA_BLOCK)
    s = _mm_nt(kbf_ref[0, pl.ds(start, MOBA_BLOCK), :], qs)
    krow = lax.broadcasted_iota(jnp.int32, s.shape, 0)
    qcol = lax.broadcasted_iota(jnp.int32, s.shape, 1)
    s = jnp.where(krow <= qcol, s, neg)
    m = jnp.max(s, axis=0, keepdims=True)
    p = jnp.exp(s - m)
    l = jnp.sum(p, axis=0, keepdims=True)
    acc = jnp.dot(vt_ref[0, i], _bf(p), preferred_element_type=F32)

    def body(j, carry):
        m, l, acc = carry
        st = pl.multiple_of(j * MOBA_BLOCK, MOBA_BLOCK)
        s = _mm_nt(kbf_ref[0, pl.ds(st, MOBA_BLOCK), :], qs)
        s = jnp.where(sel_ref[pl.ds(j, 1), :] > 0.0, s, neg)
        m_new = jnp.maximum(m, jnp.max(s, axis=0, keepdims=True))
        alpha = jnp.exp(m - m_new)
        p = jnp.exp(s - m_new)
        l = alpha * l + jnp.sum(p, axis=0, keepdims=True)
        acc = alpha * acc + jnp.dot(vt_ref[0, j], _bf(p), preferred_element_type=F32)
        return m_new, l, acc

    m, l, acc = lax.fori_loop(0, i, body, (m, l, acc))
    o_ref[0] = _bf((acc / l).T)


def _moba_prompt(q, kbf, vt, kmean):
    B, S, _ = q.shape
    nb = S // MOBA_BLOCK
    k_top = min(MOBA_TOPK, nb - 1)
    return pl.pallas_call(
        functools.partial(_moba_prompt_kernel, k_top=k_top),
        grid=(B, N_HEADS, nb),
        in_specs=[
            pl.BlockSpec((1, MOBA_BLOCK, HEAD_DIM), lambda b, h, i: (b, i, h)),
            pl.BlockSpec((1, S, HEAD_DIM), lambda b, h, i: (b, 0, h)),
            pl.BlockSpec((1, nb, HEAD_DIM, MOBA_BLOCK), lambda b, h, i: (b, 0, h, 0)),
            pl.BlockSpec((1, nb, HEAD_DIM), lambda b, h, i: (b, 0, h)),
        ],
        out_specs=pl.BlockSpec((1, MOBA_BLOCK, HEAD_DIM), lambda b, h, i: (b, i, h)),
        out_shape=jax.ShapeDtypeStruct((B, S, MIX_HALF), BF16),
        scratch_shapes=[pltpu.VMEM((nb, MOBA_BLOCK), F32)],
        compiler_params=pltpu.CompilerParams(dimension_semantics=("arbitrary", "arbitrary", "arbitrary"),
                                             vmem_limit_bytes=V7X_VMEM_LIMIT),
        name="moba_prompt",
    )(q, kbf, vt, kmean)


def _moba_sample_kernel(pt_ref, q_ref, kn_ref, vn_ref, *refs, pages_per_step, n_steps, k_top):
    kp_refs = refs[:pages_per_step]
    vp_refs = refs[pages_per_step:2 * pages_per_step]
    o_ref = refs[2 * pages_per_step]
    qbd_ref, s_ref, ksum_ref, acc_ref, l_ref = refs[2 * pages_per_step + 1:]
    del pt_ref
    step = pl.program_id(1)
    T = q_ref.shape[1]
    npair = N_HEADS * T
    pages_per_blk = MOBA_BLOCK // PAGE_SIZE
    neg = jnp.float32(-jnp.inf)
    pair_head = lax.broadcasted_iota(jnp.int32, (npair, MIX_HALF), 0) // T
    lane_head = lax.broadcasted_iota(jnp.int32, (npair, MIX_HALF), 1) // HEAD_DIM

    @pl.when(step == 0)
    def _():
        q4 = jnp.concatenate([q_ref[0]] * N_HEADS, axis=0)
        qbd_ref[...] = jnp.where(pair_head == lane_head, q4, 0.0)

    @pl.when(step < n_steps)
    def _():
        qs = _bf(qbd_ref[...] * (HEAD_DIM ** -0.5))
        for n in range(pages_per_step // pages_per_blk):
            ksum = jnp.zeros((1, MIX_HALF), F32)
            for pp in range(pages_per_blk):
                p = n * pages_per_blk + pp
                kp = kp_refs[p][0]
                row0 = pl.multiple_of((step * pages_per_step + p) * PAGE_SIZE, PAGE_SIZE)
                s_ref[pl.ds(row0, PAGE_SIZE), :] = _mm_nt(kp, qs)
                ksum = ksum + jnp.sum(kp, axis=0, keepdims=True)
            ksum_ref[pl.ds(step * (pages_per_step // pages_per_blk) + n, 1), :] = ksum

    @pl.when(step == n_steps - 1)
    def _():
        n_blocks = ksum_ref.shape[0]
        kmean = ksum_ref[...] * (1.0 / MOBA_BLOCK)
        sel = _top_blocks(_mm3_nt(kmean, qbd_ref[...]), n_blocks, k_top)
        qs = _bf(qbd_ref[...] * (HEAD_DIM ** -0.5))
        s_own = _mm_nt(kn_ref[0], qs)
        key_t = lax.broadcasted_iota(jnp.int32, (T, npair), 0)
        qry_t = lax.broadcasted_iota(jnp.int32, (T, npair), 1) % T
        s_own = jnp.where(key_t <= qry_t, s_own, neg)
        m = jnp.max(s_own, axis=0, keepdims=True)
        for n in range(n_blocks):
            sb = jnp.where(sel[n:n + 1, :] > 0.0, s_ref[n * MOBA_BLOCK:(n + 1) * MOBA_BLOCK, :], neg)
            m = jnp.maximum(m, jnp.max(sb, axis=0, keepdims=True))
        for n in range(n_blocks):
            rs = slice(n * MOBA_BLOCK, (n + 1) * MOBA_BLOCK)
            sb = jnp.where(sel[n:n + 1, :] > 0.0, s_ref[rs, :], neg)
            s_ref[rs, :] = jnp.exp(sb - m)
        p_own = jnp.exp(s_own - m)
        acc_ref[...] = _mm_tn(p_own, vn_ref[0])
        l_ref[...] = _mm_tn(p_own, jnp.ones((T, HEAD_DIM), F32))

    @pl.when(step >= n_steps)
    def _():
        vstep = step - n_steps
        acc = acc_ref[...]
        l = l_ref[...]
        ones = jnp.ones((PAGE_SIZE, HEAD_DIM), BF16)
        for p in range(pages_per_step):
            row0 = pl.multiple_of((vstep * pages_per_step + p) * PAGE_SIZE, PAGE_SIZE)
            pp = _bf(s_ref[pl.ds(row0, PAGE_SIZE), :])
            dn = (((0,), (0,)), ((), ()))
            acc = acc + lax.dot_general(pp, _bf(vp_refs[p][0]), dn, preferred_element_type=F32)
            l = l + lax.dot_general(pp, ones, dn, preferred_element_type=F32)
        acc_ref[...] = acc
        l_ref[...] = l

    @pl.when(step == 2 * n_steps - 1)
    def _():
        acc = acc_ref[...]
        l = l_ref[...]
        o_ref[0] = jnp.concatenate(
            [acc[h * T:(h + 1) * T, h * HEAD_DIM:(h + 1) * HEAD_DIM] / l[h * T:(h + 1) * T, :] for h in range(N_HEADS)],
            axis=1)


def _moba_sample(q, k_new, v_new, cache_k, cache_v, page_table, page_offset, past_len, pages_per_step):
    DB, T, _ = q.shape
    n_pages = page_table.shape[1]
    pages_per_blk = MOBA_BLOCK // PAGE_SIZE
    n_full = past_len // MOBA_BLOCK
    assert n_pages == n_full * pages_per_blk, "the current block must hold no cached rows"
    assert n_pages % pages_per_step == 0 and pages_per_step % pages_per_blk == 0
    n_steps = n_pages // pages_per_step
    k_top = min(MOBA_TOPK, n_full)
    npair = N_HEADS * T
    pt = (page_table.astype(jnp.int32) + page_offset).reshape(DB * n_pages)

    def k_map(p):
        return lambda b, s, pt_ref: (pt_ref[b * n_pages + jnp.minimum(s, n_steps - 1) * pages_per_step + p], 0, 0)

    def v_map(p):
        return lambda b, s, pt_ref: (pt_ref[b * n_pages + jnp.maximum(s - n_steps, 0) * pages_per_step + p], 0, 0)

    tok = pl.BlockSpec((1, T, MIX_HALF), lambda b, s, pt_ref: (b, 0, 0))
    page = (1, PAGE_SIZE, MIX_HALF)
    grid_spec = pltpu.PrefetchScalarGridSpec(
        num_scalar_prefetch=1,
        grid=(DB, 2 * n_steps),
        in_specs=[tok, tok, tok]
        + [pl.BlockSpec(page, k_map(p)) for p in range(pages_per_step)]
        + [pl.BlockSpec(page, v_map(p)) for p in range(pages_per_step)],
        out_specs=tok,
        scratch_shapes=[
            pltpu.VMEM((npair, MIX_HALF), F32),
            pltpu.VMEM((n_pages * PAGE_SIZE, npair), F32),
            pltpu.VMEM((n_full, MIX_HALF), F32),
            pltpu.VMEM((npair, MIX_HALF), F32),
            pltpu.VMEM((npair, HEAD_DIM), F32),
        ],
    )
    return pl.pallas_call(
        functools.partial(_moba_sample_kernel, pages_per_step=pages_per_step, n_steps=n_steps, k_top=k_top),
        grid_spec=grid_spec,
        out_shape=jax.ShapeDtypeStruct((DB, T, MIX_HALF), F32),
        compiler_params=pltpu.CompilerParams(dimension_semantics=("arbitrary", "arbitrary"),
                                             vmem_limit_bytes=V7X_VMEM_LIMIT),
        name="moba_sample",
    )(pt, q, k_new, v_new, *([cache_k] * pages_per_step), *([cache_v] * pages_per_step))


def _memory_kv_kernel(mem_ref, gain_ref, wk_ref, wv_ref, k_ref, v_ref):
    mn = _bf(_rms(mem_ref[0], gain_ref[...]))
    k_ref[0] = jnp.dot(mn, wk_ref[...], preferred_element_type=F32)
    v_ref[0] = jnp.dot(mn, wv_ref[...], preferred_element_type=F32)


def _memory_kv(mem, gain, w_ck, w_cv):
    B, M, D = mem.shape
    blk = pl.BlockSpec((1, M, D), lambda b: (b, 0, 0))
    out = jax.ShapeDtypeStruct((B, M, w_ck.shape[1]), F32)
    oblk = pl.BlockSpec((1, M, w_ck.shape[1]), lambda b: (b, 0, 0))
    return pl.pallas_call(
        _memory_kv_kernel,
        grid=(B,),
        in_specs=[blk, _full((1, D)), _full(w_ck.shape), _full(w_cv.shape)],
        out_specs=(oblk, oblk),
        out_shape=(out, out),
        compiler_params=pltpu.CompilerParams(dimension_semantics=("arbitrary",), vmem_limit_bytes=V7X_VMEM_LIMIT),
        name="memory_kv",
    )(mem, gain, w_ck, w_cv)


def _mix_kernel(x_ref, att_ref, hout_ref, wo_ref, gpost_ref, gpre_ref, wq_ref, h_ref, q_ref):
    half = att_ref.shape[-1]
    mix = (jnp.dot(_bf(att_ref[...]), wo_ref[0:half, :], preferred_element_type=F32)
           + jnp.dot(hout_ref[...], wo_ref[half:, :], preferred_element_type=F32))
    h = x_ref[...] + _rms(mix, gpost_ref[...])
    h_ref[...] = h
    q = jnp.dot(_bf(_rms(h, gpre_ref[...])), wq_ref[...], preferred_element_type=F32)
    q_ref[...] = _bf(q * (q.shape[-1] // XA_HEADS) ** -0.5)


def _mix(x, att, hout, w_out, g_post, g_pre_x, w_cq, tm):
    N, D = x.shape
    half = att.shape[1]
    return pl.pallas_call(
        _mix_kernel,
        grid=(N // tm,),
        in_specs=[
            pl.BlockSpec((tm, D), lambda i: (i, 0)),
            pl.BlockSpec((tm, half), lambda i: (i, 0)),
            pl.BlockSpec((tm, half), lambda i: (i, 0)),
            _full(w_out.shape), _full((1, D)), _full((1, D)), _full(w_cq.shape),
        ],
        out_specs=(pl.BlockSpec((tm, D), lambda i: (i, 0)), pl.BlockSpec((tm, w_cq.shape[1]), lambda i: (i, 0))),
        out_shape=(jax.ShapeDtypeStruct((N, D), F32), jax.ShapeDtypeStruct((N, w_cq.shape[1]), BF16)),
        compiler_params=pltpu.CompilerParams(dimension_semantics=("arbitrary",), vmem_limit_bytes=V7X_VMEM_LIMIT),
        name="mix_out",
    )(x, att, hout, w_out, g_post, g_pre_x, w_cq)


def _xattn_kernel(q_ref, mk_ref, mv_ref, o_ref):
    groups = q_ref.shape[0]
    dh = q_ref.shape[-1] // XA_HEADS
    for gi in range(groups):
        outs = []
        for h in range(XA_HEADS):
            hs = slice(h * dh, (h + 1) * dh)
            s = _mm_nt(q_ref[gi, :, hs], mk_ref[gi, :, hs])
            p = jnp.exp(s - jnp.max(s, axis=-1, keepdims=True))
            l = jnp.sum(p, axis=-1, keepdims=True)
            outs.append(_mm(p, mv_ref[gi, :, hs]) / l)
        o_ref[gi] = _bf(jnp.concatenate(outs, axis=1))


def _xattn(q, mk, mv, groups_per_tile, rows_per_tile):
    NG, R, W = q.shape
    M = mk.shape[1]
    return pl.pallas_call(
        _xattn_kernel,
        grid=(NG // groups_per_tile, R // rows_per_tile),
        in_specs=[
            pl.BlockSpec((groups_per_tile, rows_per_tile, W), lambda g, r: (g, r, 0)),
            pl.BlockSpec((groups_per_tile, M, W), lambda g, r: (g, 0, 0)),
            pl.BlockSpec((groups_per_tile, M, W), lambda g, r: (g, 0, 0)),
        ],
        out_specs=pl.BlockSpec((groups_per_tile, rows_per_tile, W), lambda g, r: (g, r, 0)),
        out_shape=jax.ShapeDtypeStruct((NG, R, W), BF16),
        compiler_params=pltpu.CompilerParams(dimension_semantics=("arbitrary", "arbitrary"),
                                             vmem_limit_bytes=V7X_VMEM_LIMIT),
        name="cross_attn",
    )(q, mk, mv)


def _ffn_kernel(h_ref, o_ref, wco_ref, gpx_ref, gpf_ref, wup_ref, wdn_ref, gpo_ref, y_ref, *, ff_chunk):
    h = h_ref[...] + _rms(jnp.dot(o_ref[...], wco_ref[...], preferred_element_type=F32), gpx_ref[...])
    hn = _bf(_rms(h, gpf_ref[...]))
    acc = jnp.zeros(h.shape, F32)
    for c in range(wup_ref.shape[1] // ff_chunk):
        cs = slice(c * ff_chunk, (c + 1) * ff_chunk)
        up = jnp.maximum(jnp.dot(hn, wup_ref[:, cs], preferred_element_type=F32), 0.0)
        acc = acc + jnp.dot(_bf(up * up), wdn_ref[cs, :], preferred_element_type=F32)
    y_ref[...] = h + _rms(acc, gpo_ref[...])


def _ffn(h, o, w_co, g_post_x, g_pre_ffn, w_up, w_down, g_post_ffn, tm, ff_chunk):
    N, D = h.shape
    tok = pl.BlockSpec((tm, D), lambda i: (i, 0))
    return pl.pallas_call(
        functools.partial(_ffn_kernel, ff_chunk=ff_chunk),
        grid=(N // tm,),
        in_specs=[tok, pl.BlockSpec((tm, o.shape[1]), lambda i: (i, 0)),
                  _full(w_co.shape), _full((1, D)), _full((1, D)), _full(w_up.shape), _full(w_down.shape),
                  _full((1, D))],
        out_specs=tok,
        out_shape=jax.ShapeDtypeStruct((N, D), F32),
        compiler_params=pltpu.CompilerParams(dimension_semantics=("arbitrary",), vmem_limit_bytes=V7X_VMEM_LIMIT),
        name="co_ffn",
    )(h, o, w_co, g_post_x, g_pre_ffn, w_up, w_down, g_post_ffn)


def _tiles(n_prompt_seq):
    ts = MOBA_BLOCK
    tm = min(512, n_prompt_seq)
    return ts, tm


def kernel(x_prompt, x_sample, cache_k, cache_v, state_hgrn, cache_mem_k, cache_mem_v, page_table, mem_prompt, norm_pre_mix, w_in, hgrn_lb, hgrn_norm, w_out, norm_post_mix, norm_mem, norm_pre_x, w_cq, w_ck, w_cv, w_co, norm_post_x, norm_pre_ffn, w_up, w_down, norm_post_ffn):
    B, S, D = x_prompt.shape
    DB, T, _ = x_sample.shape
    depth = w_in.shape[0]
    n_pool = cache_k.shape[1]
    past_len = page_table.shape[1] * PAGE_SIZE
    assert D == 2 * MIX_HALF and S % MOBA_BLOCK == 0 and cache_k.shape[2] == PAGE_SIZE
    ts, tm = _tiles(S)
    tm_s = min(tm, DB * T)
    seqs_per_tile = min(8, DB)
    xa_w = w_cq.shape[2]

    lb_all = jnp.cumsum(jax.nn.softmax(hgrn_lb.astype(F32), axis=0), axis=0)
    cache_k2 = cache_k.reshape(depth * n_pool, PAGE_SIZE, MIX_HALF)
    cache_v2 = cache_v.reshape(depth * n_pool, PAGE_SIZE, MIX_HALF)
    row = lambda a: a.reshape(1, -1)

    yp, ys = x_prompt, x_sample
    kp_l, vp_l, sp_l, mkp_l, mvp_l, ks_l, vs_l, ss_l = [], [], [], [], [], [], [], []
    for l in range(depth):
        w_qk = w_in[l][:, :2 * MIX_HALF]
        wqk_hi = _bf(w_qk)
        wqk_lo = _bf(w_qk - wqk_hi.astype(F32))
        w_rest = _bf(w_in[l][:, 2 * MIX_HALF:])
        lb = row(lb_all[l])
        hgain = row(hgrn_norm[l])
        wo_b, wcq_b, wck_b, wcv_b, wco_b = _bf(w_out[l]), _bf(w_cq[l]), _bf(w_ck[l]), _bf(w_cv[l]), _bf(w_co[l])
        wup_b, wdn_b = _bf(w_up[l]), _bf(w_down[l])

        def trunk_tail(x2d, att2d, hout2d, q_groups, mk, mv, groups_per_tile, rows_per_tile, tile):
            h1, qx = _mix(x2d, att2d, hout2d, wo_b, row(norm_post_mix[l]), row(norm_pre_x[l]), wcq_b, tile)
            ox = _xattn(qx.reshape(q_groups + (xa_w,)), mk, mv, groups_per_tile, rows_per_tile)
            return _ffn(h1, ox.reshape(-1, xa_w), wco_b, row(norm_post_x[l]), row(norm_pre_ffn[l]), wup_b, wdn_b,
                        row(norm_post_ffn[l]), tile, 1024)

        q, k, v, kbf, vt, kmean, hout, s_fin = _proj_prompt(yp, row(norm_pre_mix[l]), wqk_hi, wqk_lo, w_rest, lb, hgain, ts)
        att = _moba_prompt(q, kbf, vt, kmean)
        mkp, mvp = _memory_kv(mem_prompt, row(norm_mem[l]), wck_b, wcv_b)
        yp = trunk_tail(yp.reshape(B * S, D), att.reshape(B * S, MIX_HALF), hout.reshape(B * S, MIX_HALF),
                        (B, S), mkp, mvp, 1, tm, tm).reshape(B, S, D)
        kp_l.append(k.reshape(B, S, N_HEADS, HEAD_DIM))
        vp_l.append(v.reshape(B, S, N_HEADS, HEAD_DIM))
        sp_l.append(s_fin)
        mkp_l.append(mkp.reshape(B, -1, XA_HEADS, xa_w // XA_HEADS))
        mvp_l.append(mvp.reshape(B, -1, XA_HEADS, xa_w // XA_HEADS))

        q, k, v, hout, s_new = _proj_sample(ys, row(norm_pre_mix[l]), wqk_hi, wqk_lo, w_rest, lb, hgain,
                                            state_hgrn[l], past_len, seqs_per_tile)
        att = _moba_sample(q.reshape(DB, T, MIX_HALF), k.reshape(DB, T, MIX_HALF), v.reshape(DB, T, MIX_HALF),
                           cache_k2, cache_v2, page_table, l * n_pool, past_len, 8)
        M = cache_mem_k.shape[2]
        ys = trunk_tail(ys.reshape(DB * T, D), att.reshape(DB * T, MIX_HALF), hout, (DB, T),
                        cache_mem_k[l].reshape(DB, M, xa_w), cache_mem_v[l].reshape(DB, M, xa_w),
                        min(4, DB), T, tm_s).reshape(DB, T, D)
        ks_l.append(k.reshape(DB, T, N_HEADS, HEAD_DIM))
        vs_l.append(v.reshape(DB, T, N_HEADS, HEAD_DIM))
        ss_l.append(s_new)
    return (yp, ys, jnp.stack(kp_l), jnp.stack(vp_l), jnp.stack(sp_l), jnp.stack(mkp_l), jnp.stack(mvp_l),
            jnp.stack(ks_l), jnp.stack(vs_l), jnp.stack(ss_l))
```

```python
import functools

import jax
import jax.numpy as jnp
from jax import lax
from jax.experimental import pallas as pl
from jax.experimental.pallas import tpu as pltpu

F32 = jnp.float32
BF16 = jnp.bfloat16

NORM_EPS = 1e-6
ROPE_THETA = 10000.0
HEAD_DIM = 128
N_HEADS = 4
MIX_HALF = N_HEADS * HEAD_DIM
MOBA_BLOCK = 256
MOBA_TOPK = 3
PAGE_SIZE = 128
HGRN_CHUNK = 64
HGRN_SUB = 16
XA_HEADS = 4
V7X_VMEM_LIMIT = 56 * 1024 * 1024


def _bf(x):
    return x.astype(BF16)


def _mm(a, b):
    return jnp.dot(_bf(a), _bf(b), preferred_element_type=F32)


def _mm_nt(a, b):
    return lax.dot_general(_bf(a), _bf(b), (((1,), (1,)), ((), ())), preferred_element_type=F32)


def _mm_tn(a, b):
    return lax.dot_general(_bf(a), _bf(b), (((0,), (0,)), ((), ())), preferred_element_type=F32)


def _split2(x):
    hi = x.astype(BF16)
    lo = (x - hi.astype(F32)).astype(BF16)
    return hi, lo


def _split3(x):
    a = x.astype(BF16)
    r = x - a.astype(F32)
    b = r.astype(BF16)
    c = (r - b.astype(F32)).astype(BF16)
    return a, b, c


def _mm3_nt(a, b):
    a_hi, a_lo = _split2(a)
    b_hi, b_lo = _split2(b)
    dn = (((1,), (1,)), ((), ()))
    dot = functools.partial(lax.dot_general, dimension_numbers=dn, preferred_element_type=F32)
    return (dot(a_lo, b_hi) + dot(a_hi, b_lo)) + dot(a_hi, b_hi)


def _rms(x, gain):
    return x * lax.rsqrt(jnp.mean(x * x, axis=-1, keepdims=True) + NORM_EPS) * gain


def _silu(x):
    return x * (1.0 / (1.0 + jnp.exp(-x)))


def _sigmoid(x):
    return 1.0 / (1.0 + jnp.exp(-x))


def _cumsum_rows(g, lmat):
    g1, g2, g3 = _split3(g)
    dot = functools.partial(jnp.dot, preferred_element_type=F32)
    return (dot(lmat, g3) + dot(lmat, g2)) + dot(lmat, g1)


def _chunk_matrices(rows, chunk, sub):
    r = lax.broadcasted_iota(jnp.int32, (rows, rows), 0)
    c = lax.broadcasted_iota(jnp.int32, (rows, rows), 1)
    same_chunk = (r // chunk) == (c // chunk)
    incl = same_chunk & (c <= r)
    start = same_chunk & (c < (r // sub) * sub)
    end = same_chunk & (c < (r // sub + 1) * sub)
    as_bf = lambda m: jnp.where(m, 1.0, 0.0).astype(BF16)
    return as_bf(incl), as_bf(start), as_bf(end)


def _hgrn_chunk(q, k, v, G, Bs, Be, S, sub):
    c = q.shape[0]
    nsub = c // sub
    qh = q * jnp.exp(G - Bs)
    kd = k * jnp.exp(Bs - G)
    qg = q * jnp.exp(G)
    row = lax.broadcasted_iota(jnp.int32, (c, c), 0)
    col = lax.broadcasted_iota(jnp.int32, (c, c), 1)
    if nsub == 1:
        att = jnp.where(col <= row, _mm_nt(qh, kd), 0.0)
        o = _mm(att, v) + _mm(qg, S)
    else:
        kh = k * jnp.exp(Be - G)
        lhs = [qh]
        z = jnp.zeros((sub, HEAD_DIM), F32)
        rhs_rows = [jnp.concatenate([kd] + [jnp.zeros_like(kd)] * (nsub - 1), axis=1)]
        for j in range(nsub - 1):
            g_end_j = G[(j + 1) * sub - 1:(j + 1) * sub, :]
            lhs.append(q * jnp.exp(jnp.minimum(G - g_end_j, 0.0)))
            parts = [z] * nsub
            parts[j + 1] = kh[j * sub:(j + 1) * sub, :]
            rhs_rows.append(jnp.concatenate(parts, axis=1))
        n_off = (nsub - 1) * sub
        pad = (-(c + n_off)) % HEAD_DIM
        if pad:
            rhs_rows.append(jnp.zeros((pad, nsub * HEAD_DIM), F32))
        R = _mm_nt(jnp.concatenate(lhs, axis=1), jnp.concatenate(rhs_rows, axis=0))
        width = c + n_off + pad
        row2 = lax.broadcasted_iota(jnp.int32, (c, width), 0)
        col2 = lax.broadcasted_iota(jnp.int32, (c, width), 1)
        diag_ok = (col2 < c) & (col2 <= row2) & ((col2 // sub) == (row2 // sub))
        off_ok = (col2 >= c) & (col2 < c + n_off) & (((col2 - c) // sub) < (row2 // sub))
        att2 = jnp.where(diag_ok | off_ok, R, 0.0)
        v_off = v[:n_off, :]
        v_rows = [v, v_off]
        if pad:
            v_rows.append(jnp.zeros((pad, HEAD_DIM), F32))
        o = _mm(jnp.concatenate([att2, qg], axis=1), jnp.concatenate(v_rows + [S], axis=0))
    g_end = G[c - 1:c, :]
    kk = k * jnp.exp(g_end - G)
    decay = jnp.broadcast_to(jnp.exp(g_end), (HEAD_DIM, HEAD_DIM)).T
    S_new = decay * S + _mm_tn(kk, v)
    return o, S_new


def _rotary(x, cos, sin):
    return x * cos + pltpu.roll(x, HEAD_DIM // 2, 1) * sin


def _project(x, gain, wqk_hi, wqk_lo, w_rest, cos, sin, lb):
    hn = _rms(x, gain)
    hn_hi, hn_lo = _split2(hn)
    dot = functools.partial(jnp.dot, preferred_element_type=F32)
    qk = (dot(hn_lo, wqk_hi) + dot(hn_hi, wqk_lo)) + dot(hn_hi, wqk_hi)
    rest = dot(hn_hi, w_rest)
    q = jnp.concatenate([_rotary(qk[:, h * HEAD_DIM:(h + 1) * HEAD_DIM], cos, sin) for h in range(N_HEADS)], axis=1)
    k = jnp.concatenate([_rotary(qk[:, MIX_HALF + h * HEAD_DIM:MIX_HALF + (h + 1) * HEAD_DIM], cos, sin)
                         for h in range(N_HEADS)], axis=1)
    v = rest[:, 0:MIX_HALF]
    hq = _silu(rest[:, MIX_HALF:2 * MIX_HALF])
    f = lb + (1.0 - lb) * _sigmoid(rest[:, 2 * MIX_HALF:3 * MIX_HALF])
    hk = 1.0 - f
    g = jnp.log(f)
    hv = rest[:, 3 * MIX_HALF:4 * MIX_HALF]
    hg = rest[:, 4 * MIX_HALF:5 * MIX_HALF]
    return q, k, v, hq, hk, hv, g, hg


def _hgrn_out(o, gain, hg):
    return _rms(o, gain) * _silu(hg)


def _proj_prompt_kernel(x_ref, gain_ref, wqk_hi_ref, wqk_lo_ref, wrest_ref, cos_ref, sin_ref, lb_ref, hgain_ref,
                        q_ref, k_ref, v_ref, kbf_ref, vt_ref, kmean_ref, hout_ref, state_ref):
    t = pl.program_id(1)

    @pl.when(t == 0)
    def _():
        state_ref[...] = jnp.zeros_like(state_ref)

    ts = x_ref.shape[1]
    q, k, v, hq, hk, hv, g, hg = _project(x_ref[0], gain_ref[...], wqk_hi_ref[...], wqk_lo_ref[...], wrest_ref[...],
                                          cos_ref[...], sin_ref[...], lb_ref[...])
    q_ref[0] = q
    for h in range(N_HEADS):
        k_ref[0, :, h, :] = k[:, h * HEAD_DIM:(h + 1) * HEAD_DIM]
        v_ref[0, :, h, :] = v[:, h * HEAD_DIM:(h + 1) * HEAD_DIM]
    kbf_ref[0] = _bf(k)
    n_blk = ts // MOBA_BLOCK
    for n in range(n_blk):
        sl = slice(n * MOBA_BLOCK, (n + 1) * MOBA_BLOCK)
        vt_ref[0, n] = _bf(v[sl, :].T)
        kmean_ref[0, pl.ds(t * n_blk + n, 1), :] = jnp.mean(k[sl, :], axis=0, keepdims=True)

    l_incl, l_start, l_end = _chunk_matrices(ts, HGRN_CHUNK, HGRN_SUB)
    G = _cumsum_rows(g, l_incl)
    Bs = _cumsum_rows(g, l_start)
    Be = _cumsum_rows(g, l_end)
    hgain = hgain_ref[...]
    for h in range(N_HEADS):
        hs = slice(h * HEAD_DIM, (h + 1) * HEAD_DIM)
        S = state_ref[0, h]
        outs = []
        for ci in range(ts // HGRN_CHUNK):
            rs = slice(ci * HGRN_CHUNK, (ci + 1) * HGRN_CHUNK)
            o, S = _hgrn_chunk(hq[rs, hs], hk[rs, hs], hv[rs, hs], G[rs, hs], Bs[rs, hs], Be[rs, hs], S, HGRN_SUB)
            outs.append(o)
        state_ref[0, h] = S
        hout_ref[0, :, hs] = _bf(_hgrn_out(jnp.concatenate(outs, axis=0), hgain, hg[:, hs]))


def _proj_sample_kernel(x_ref, gain_ref, wqk_hi_ref, wqk_lo_ref, wrest_ref, cos_ref, sin_ref, lb_ref, hgain_ref,
                        s0_ref, q_ref, k_ref, v_ref, hout_ref, state_ref, *, seq_len):
    rows = x_ref.shape[0]
    q, k, v, hq, hk, hv, g, hg = _project(x_ref[...], gain_ref[...], wqk_hi_ref[...], wqk_lo_ref[...], wrest_ref[...],
                                          cos_ref[...], sin_ref[...], lb_ref[...])
    q_ref[...] = q
    for si in range(rows // seq_len):
        rs = slice(si * seq_len, (si + 1) * seq_len)
        for h in range(N_HEADS):
            k_ref[si, :, h, :] = k[rs, h * HEAD_DIM:(h + 1) * HEAD_DIM]
            v_ref[si, :, h, :] = v[rs, h * HEAD_DIM:(h + 1) * HEAD_DIM]
    l_incl, _, _ = _chunk_matrices(rows, seq_len, seq_len)
    G = _cumsum_rows(g, l_incl)
    zero = jnp.zeros((seq_len, HEAD_DIM), F32)
    hgain = hgain_ref[...]
    for h in range(N_HEADS):
        hs = slice(h * HEAD_DIM, (h + 1) * HEAD_DIM)
        outs = []
        for si in range(rows // seq_len):
            rs = slice(si * seq_len, (si + 1) * seq_len)
            o, S = _hgrn_chunk(hq[rs, hs], hk[rs, hs], hv[rs, hs], G[rs, hs], zero, zero, s0_ref[si, h], seq_len)
            state_ref[si, h] = S
            outs.append(o)
        hout_ref[:, hs] = _bf(_hgrn_out(jnp.concatenate(outs, axis=0), hgain, hg[:, hs]))


def _rotary_tables(pos):
    half = HEAD_DIM // 2
    inv_freq = ROPE_THETA ** (-jnp.arange(half, dtype=F32) / half)
    ang = pos.astype(F32)[:, None] * inv_freq[None, :]
    cos = jnp.cos(ang)
    sin = jnp.sin(ang)
    return jnp.concatenate([cos, cos], axis=1), jnp.concatenate([-sin, sin], axis=1)


def _full(shape):
    return pl.BlockSpec(shape, lambda *_: (0,) * len(shape))


def _proj_prompt(x, gain, wqk_hi, wqk_lo, w_rest, lb, hgain, ts):
    B, S, D = x.shape
    nb = S // MOBA_BLOCK
    cos, sin = _rotary_tables(jnp.arange(S))
    out_shape = (
        jax.ShapeDtypeStruct((B, S, MIX_HALF), F32),
        jax.ShapeDtypeStruct((B, S, N_HEADS, HEAD_DIM), F32),
        jax.ShapeDtypeStruct((B, S, N_HEADS, HEAD_DIM), F32),
        jax.ShapeDtypeStruct((B, S, MIX_HALF), BF16),
        jax.ShapeDtypeStruct((B, nb, MIX_HALF, MOBA_BLOCK), BF16),
        jax.ShapeDtypeStruct((B, nb, MIX_HALF), F32),
        jax.ShapeDtypeStruct((B, S, MIX_HALF), BF16),
        jax.ShapeDtypeStruct((B, N_HEADS, HEAD_DIM, HEAD_DIM), F32),
    )
    tok = pl.BlockSpec((1, ts, MIX_HALF), lambda b, t: (b, t, 0))
    tok_heads = pl.BlockSpec((1, ts, N_HEADS, HEAD_DIM), lambda b, t: (b, t, 0, 0))
    return pl.pallas_call(
        _proj_prompt_kernel,
        grid=(B, S // ts),
        in_specs=[
            pl.BlockSpec((1, ts, D), lambda b, t: (b, t, 0)),
            _full((1, D)), _full(wqk_hi.shape), _full(wqk_lo.shape), _full(w_rest.shape),
            pl.BlockSpec((ts, HEAD_DIM), lambda b, t: (t, 0)),
            pl.BlockSpec((ts, HEAD_DIM), lambda b, t: (t, 0)),
            _full((1, MIX_HALF)), _full((1, HEAD_DIM)),
        ],
        out_specs=(
            tok, tok_heads, tok_heads, tok,
            pl.BlockSpec((1, ts // MOBA_BLOCK, MIX_HALF, MOBA_BLOCK), lambda b, t: (b, t, 0, 0)),
            pl.BlockSpec((1, nb, MIX_HALF), lambda b, t: (b, 0, 0)),
            tok,
            pl.BlockSpec((1, N_HEADS, HEAD_DIM, HEAD_DIM), lambda b, t: (b, 0, 0, 0)),
        ),
        out_shape=out_shape,
        compiler_params=pltpu.CompilerParams(dimension_semantics=("arbitrary", "arbitrary"),
                                             vmem_limit_bytes=V7X_VMEM_LIMIT),
        name="proj_prompt",
    )(x, gain, wqk_hi, wqk_lo, w_rest, cos, sin, lb, hgain)


def _proj_sample(x, gain, wqk_hi, wqk_lo, w_rest, lb, hgain, state0, past_len, seqs_per_tile):
    DB, T, D = x.shape
    rows = seqs_per_tile * T
    cos, sin = _rotary_tables(past_len + jnp.arange(T))
    cos = jnp.tile(cos, (seqs_per_tile, 1))
    sin = jnp.tile(sin, (seqs_per_tile, 1))
    out_shape = (
        jax.ShapeDtypeStruct((DB * T, MIX_HALF), F32),
        jax.ShapeDtypeStruct((DB, T, N_HEADS, HEAD_DIM), F32),
        jax.ShapeDtypeStruct((DB, T, N_HEADS, HEAD_DIM), F32),
        jax.ShapeDtypeStruct((DB * T, MIX_HALF), BF16),
        jax.ShapeDtypeStruct((DB, N_HEADS, HEAD_DIM, HEAD_DIM), F32),
    )
    tok = pl.BlockSpec((rows, MIX_HALF), lambda i: (i, 0))
    tok_heads = pl.BlockSpec((seqs_per_tile, T, N_HEADS, HEAD_DIM), lambda i: (i, 0, 0, 0))
    st = pl.BlockSpec((seqs_per_tile, N_HEADS, HEAD_DIM, HEAD_DIM), lambda i: (i, 0, 0, 0))
    return pl.pallas_call(
        functools.partial(_proj_sample_kernel, seq_len=T),
        grid=(DB // seqs_per_tile,),
        in_specs=[
            pl.BlockSpec((rows, D), lambda i: (i, 0)),
            _full((1, D)), _full(wqk_hi.shape), _full(wqk_lo.shape), _full(w_rest.shape),
            _full((rows, HEAD_DIM)), _full((rows, HEAD_DIM)),
            _full((1, MIX_HALF)), _full((1, HEAD_DIM)),
            st,
        ],
        out_specs=(tok, tok_heads, tok_heads, tok, st),
        out_shape=out_shape,
        compiler_params=pltpu.CompilerParams(dimension_semantics=("arbitrary",),
                                             vmem_limit_bytes=V7X_VMEM_LIMIT),
        name="proj_sample",
    )(x.reshape(DB * T, D), gain, wqk_hi, wqk_lo, w_rest, cos, sin, lb, hgain, state0)


def _top_blocks(gate, n_valid, k_top, axis):
    nb = gate.shape[axis]
    blk = lax.broadcasted_iota(jnp.int32, gate.shape, axis)
    neg = jnp.float32(-jnp.inf)
    g = jnp.where(blk < n_valid, gate, neg)
    sel = jnp.zeros(gate.shape, F32)
    for _ in range(k_top):
        m = jnp.max(g, axis=axis, keepdims=True)
        first = jnp.min(jnp.where(g == m, blk, nb), axis=axis, keepdims=True)
        pick = blk == first
        sel = jnp.where(pick & (m > neg), 1.0, sel)
        g = jnp.where(pick, neg, g)
    return sel


def _moba_prompt_kernel(q_ref, kbf_ref, vt_ref, kmean_ref, o_ref, sel_ref, *, k_top, group):
    i = pl.program_id(2)
    q = q_ref[0]
    sel_ref[...] = _top_blocks(_mm3_nt(kmean_ref[0], q), i, k_top, 0)
    qs = _bf(q * (HEAD_DIM ** -0.5))
    neg = jnp.float32(-jnp.inf)

    start = pl.multiple_of(i * MOBA_BLOCK, MOBA_BLOCK)
    s = _mm_nt(kbf_ref[0, pl.ds(start, MOBA_BLOCK), :], qs)
    krow = lax.broadcasted_iota(jnp.int32, s.shape, 0)
    qcol = lax.broadcasted_iota(jnp.int32, s.shape, 1)
    s = jnp.where(krow <= qcol, s, neg)
    m = jnp.max(s, axis=0, keepdims=True)
    p = jnp.exp(s - m)
    l = jnp.sum(p, axis=0, keepdims=True)
    acc = jnp.dot(vt_ref[0, i], _bf(p), preferred_element_type=F32)

    def body(g, carry):
        m, l, acc = carry
        st = pl.multiple_of(g * (group * MOBA_BLOCK), group * MOBA_BLOCK)
        s = _mm_nt(kbf_ref[0, pl.ds(st, group * MOBA_BLOCK), :], qs)
        parts = []
        m_new = m
        for u in range(group):
            su = jnp.where(sel_ref[pl.ds(g * group + u, 1), :] > 0.0, s[u * MOBA_BLOCK:(u + 1) * MOBA_BLOCK, :], neg)
            m_new = jnp.maximum(m_new, jnp.max(su, axis=0, keepdims=True))
            parts.append(su)
        alpha = jnp.exp(m - m_new)
        l = alpha * l
        acc = alpha * acc
        for u in range(group):
            p = jnp.exp(parts[u] - m_new)
            l = l + jnp.sum(p, axis=0, keepdims=True)
            acc = acc + jnp.dot(vt_ref[0, g * group + u], _bf(p), preferred_element_type=F32)
        return m_new, l, acc

    m, l, acc = lax.fori_loop(0, (i + group - 1) // group, body, (m, l, acc))
    o_ref[0] = _bf((acc / l).T)


def _moba_prompt(q, kbf, vt, kmean):
    B, S, _ = q.shape
    nb = S // MOBA_BLOCK
    k_top = min(MOBA_TOPK, nb - 1)
    group = 4 if nb % 4 == 0 else (2 if nb % 2 == 0 else 1)
    return pl.pallas_call(
        functools.partial(_moba_prompt_kernel, k_top=k_top, group=group),
        grid=(B, N_HEADS, nb),
        in_specs=[
            pl.BlockSpec((1, MOBA_BLOCK, HEAD_DIM), lambda b, h, i: (b, i, h)),
            pl.BlockSpec((1, S, HEAD_DIM), lambda b, h, i: (b, 0, h)),
            pl.BlockSpec((1, nb, HEAD_DIM, MOBA_BLOCK), lambda b, h, i: (b, 0, h, 0)),
            pl.BlockSpec((1, nb, HEAD_DIM), lambda b, h, i: (b, 0, h)),
        ],
        out_specs=pl.BlockSpec((1, MOBA_BLOCK, HEAD_DIM), lambda b, h, i: (b, i, h)),
        out_shape=jax.ShapeDtypeStruct((B, S, MIX_HALF), BF16),
        scratch_shapes=[pltpu.VMEM((nb, MOBA_BLOCK), F32)],
        compiler_params=pltpu.CompilerParams(dimension_semantics=("arbitrary", "arbitrary", "arbitrary"),
                                             vmem_limit_bytes=V7X_VMEM_LIMIT),
        name="moba_prompt",
    )(q, kbf, vt, kmean)


def _heads_to_lanes(ref, *lead):
    return jnp.concatenate([ref[lead + (slice(None), h, slice(None))] for h in range(N_HEADS)], axis=1)


def _page_to_lanes(ref):
    return jnp.concatenate([ref[pl.ds(h, PAGE_SIZE, stride=N_HEADS), :] for h in range(N_HEADS)], axis=1)


def _moba_sample_scores_kernel(pt_ref, q_ref, kn_ref, vn_ref, expand_ref, *refs, pages_per_step, n_steps, k_top):
    kp_refs = refs[:pages_per_step]
    p_ref, acc0_ref, linv_ref = refs[pages_per_step:pages_per_step + 3]
    qbd_ref, s_ref, ksum_ref = refs[pages_per_step + 3:]
    del pt_ref
    step = pl.program_id(1)
    T = q_ref.shape[1]
    npair = N_HEADS * T
    n_pages = s_ref.shape[0]
    pages_per_blk = MOBA_BLOCK // PAGE_SIZE
    blks_per_step = pages_per_step // pages_per_blk
    neg = jnp.float32(-jnp.inf)

    @pl.when(step == 0)
    def _():
        pair_head = lax.broadcasted_iota(jnp.int32, (npair, MIX_HALF), 0) // T
        lane_head = lax.broadcasted_iota(jnp.int32, (npair, MIX_HALF), 1) // HEAD_DIM
        q4 = jnp.concatenate([q_ref[0]] * N_HEADS, axis=0)
        qbd_ref[...] = jnp.where(pair_head == lane_head, q4, 0.0)

    qs = _bf(qbd_ref[...] * (HEAD_DIM ** -0.5))
    for n in range(blks_per_step):
        ksum = jnp.zeros((1, MIX_HALF), F32)
        for pp in range(pages_per_blk):
            p = n * pages_per_blk + pp
            kp = _page_to_lanes(kp_refs[p])
            s_ref[step * pages_per_step + p] = _mm_nt(qs, kp)
            ksum = ksum + jnp.sum(kp, axis=0, keepdims=True)
        ksum_ref[pl.ds(step * blks_per_step + n, 1), :] = ksum

    @pl.when(step == n_steps - 1)
    def _():
        n_blocks = ksum_ref.shape[0]
        kmean = ksum_ref[...] * (1.0 / MOBA_BLOCK)
        sel = _top_blocks(_mm3_nt(qbd_ref[...], kmean), n_blocks, k_top, 1)
        chosen = jnp.dot(_bf(sel), expand_ref[...], preferred_element_type=F32)
        s_own = _mm_nt(qs, _heads_to_lanes(kn_ref, 0))
        key_t = lax.broadcasted_iota(jnp.int32, (npair, T), 1)
        qry_t = lax.broadcasted_iota(jnp.int32, (npair, T), 0) % T
        s_own = jnp.where(key_t <= qry_t, s_own, neg)

        def masked(p):
            return jnp.where(chosen[:, p * PAGE_SIZE:(p + 1) * PAGE_SIZE] > 0.5, s_ref[p], neg)

        m_lanes = masked(0)
        for p in range(1, n_pages):
            m_lanes = jnp.maximum(m_lanes, masked(p))
        m = jnp.maximum(jnp.max(m_lanes, axis=1, keepdims=True), jnp.max(s_own, axis=1, keepdims=True))
        l_lanes = jnp.zeros((npair, PAGE_SIZE), F32)
        for p in range(n_pages):
            w = jnp.exp(masked(p) - m)
            l_lanes = l_lanes + w
            p_ref[0, p] = _bf(w)
        w_own = jnp.exp(s_own - m)
        l = jnp.sum(l_lanes, axis=1, keepdims=True) + jnp.sum(w_own, axis=1, keepdims=True)
        acc0_ref[0] = _mm(w_own, _heads_to_lanes(vn_ref, 0))
        linv_ref[0] = jnp.broadcast_to(1.0 / l, (npair, HEAD_DIM))


def _moba_sample_values_kernel(pt_ref, p_ref, acc0_ref, linv_ref, *refs, pages_per_step, n_steps):
    vp_refs = refs[:pages_per_step]
    o_ref = refs[pages_per_step]
    acc_ref = refs[pages_per_step + 1]
    del pt_ref
    step = pl.program_id(1)
    T = o_ref.shape[1]

    @pl.when(step == 0)
    def _():
        acc_ref[...] = acc0_ref[0]

    acc = acc_ref[...]
    for p in range(pages_per_step):
        vp = _bf(_page_to_lanes(vp_refs[p]))
        acc = acc + jnp.dot(p_ref[0, step * pages_per_step + p], vp, preferred_element_type=F32)
    acc_ref[...] = acc

    @pl.when(step == n_steps - 1)
    def _():
        linv = linv_ref[0]
        o_ref[0] = jnp.concatenate(
            [acc[h * T:(h + 1) * T, h * HEAD_DIM:(h + 1) * HEAD_DIM] * linv[h * T:(h + 1) * T, :] for h in range(N_HEADS)],
            axis=1)


def _moba_sample(q, k_new, v_new, cache_k, cache_v, page_table, layer, past_len, pages_per_step):
    DB, T, _ = q.shape
    n_pages = page_table.shape[1]
    pages_per_blk = MOBA_BLOCK // PAGE_SIZE
    n_full = past_len // MOBA_BLOCK
    assert n_pages == n_full * pages_per_blk, "the current block must hold no cached rows"
    assert n_pages % pages_per_step == 0 and pages_per_step % pages_per_blk == 0
    n_steps = n_pages // pages_per_step
    k_top = min(MOBA_TOPK, n_full)
    npair = N_HEADS * T
    n_keys = n_pages * PAGE_SIZE
    pt = page_table.astype(jnp.int32).reshape(DB * n_pages)
    expand = (jnp.arange(n_keys)[None, :] // MOBA_BLOCK == jnp.arange(n_full)[:, None]).astype(BF16)

    def page_map(p):
        return lambda b, s, pt_ref: (layer * n_pool + pt_ref[b * n_pages + s * pages_per_step + p], 0)

    per_seq = lambda *tail: (lambda b, s, pt_ref: (b,) + tail)
    page = (PAGE_SIZE * N_HEADS, HEAD_DIM)
    n_pool = cache_k.shape[1]
    cache_k = cache_k.reshape(-1, HEAD_DIM)
    cache_v = cache_v.reshape(-1, HEAD_DIM)
    new_tok = pl.BlockSpec((1, T, N_HEADS, HEAD_DIM), per_seq(0, 0, 0))
    p_spec = pl.BlockSpec((1, n_pages, npair, PAGE_SIZE), per_seq(0, 0, 0))
    acc_spec = pl.BlockSpec((1, npair, MIX_HALF), per_seq(0, 0))
    linv_spec = pl.BlockSpec((1, npair, HEAD_DIM), per_seq(0, 0))
    params = pltpu.CompilerParams(dimension_semantics=("arbitrary", "arbitrary"), vmem_limit_bytes=V7X_VMEM_LIMIT)

    weights, acc0, linv = pl.pallas_call(
        functools.partial(_moba_sample_scores_kernel, pages_per_step=pages_per_step, n_steps=n_steps, k_top=k_top),
        grid_spec=pltpu.PrefetchScalarGridSpec(
            num_scalar_prefetch=1,
            grid=(DB, n_steps),
            in_specs=[pl.BlockSpec((1, T, MIX_HALF), per_seq(0, 0)), new_tok, new_tok,
                      pl.BlockSpec((n_full, n_keys), lambda b, s, pt_ref: (0, 0))]
            + [pl.BlockSpec(page, page_map(p)) for p in range(pages_per_step)],
            out_specs=(p_spec, acc_spec, linv_spec),
            scratch_shapes=[
                pltpu.VMEM((npair, MIX_HALF), F32),
                pltpu.VMEM((n_pages, npair, PAGE_SIZE), F32),
                pltpu.VMEM((n_full, MIX_HALF), F32),
            ],
        ),
        out_shape=(jax.ShapeDtypeStruct((DB, n_pages, npair, PAGE_SIZE), BF16),
                   jax.ShapeDtypeStruct((DB, npair, MIX_HALF), F32),
                   jax.ShapeDtypeStruct((DB, npair, HEAD_DIM), F32)),
        compiler_params=params,
        name="moba_sample_scores",
    )(pt, q, k_new, v_new, expand, *([cache_k] * pages_per_step))

    return pl.pallas_call(
        functools.partial(_moba_sample_values_kernel, pages_per_step=pages_per_step, n_steps=n_steps),
        grid_spec=pltpu.PrefetchScalarGridSpec(
            num_scalar_prefetch=1,
            grid=(DB, n_steps),
            in_specs=[p_spec, acc_spec, linv_spec] + [pl.BlockSpec(page, page_map(p)) for p in range(pages_per_step)],
            out_specs=pl.BlockSpec((1, T, MIX_HALF), per_seq(0, 0)),
            scratch_shapes=[pltpu.VMEM((npair, MIX_HALF), F32)],
        ),
        out_shape=jax.ShapeDtypeStruct((DB, T, MIX_HALF), F32),
        compiler_params=params,
        name="moba_sample_values",
    )(pt, weights, acc0, linv, *([cache_v] * pages_per_step))


def _memory_kv_kernel(mem_ref, gain_ref, wk_ref, wv_ref, k_ref, v_ref):
    mn = _bf(_rms(mem_ref[0], gain_ref[...]))
    k = jnp.dot(mn, wk_ref[...], preferred_element_type=F32)
    v = jnp.dot(mn, wv_ref[...], preferred_element_type=F32)
    dh = k_ref.shape[-1]
    for h in range(XA_HEADS):
        k_ref[0, :, h, :] = k[:, h * dh:(h + 1) * dh]
        v_ref[0, :, h, :] = v[:, h * dh:(h + 1) * dh]


def _memory_kv(mem, gain, w_ck, w_cv):
    B, M, D = mem.shape
    dh = w_ck.shape[1] // XA_HEADS
    blk = pl.BlockSpec((1, M, D), lambda b: (b, 0, 0))
    out = jax.ShapeDtypeStruct((B, M, XA_HEADS, dh), F32)
    oblk = pl.BlockSpec((1, M, XA_HEADS, dh), lambda b: (b, 0, 0, 0))
    return pl.pallas_call(
        _memory_kv_kernel,
        grid=(B,),
        in_specs=[blk, _full((1, D)), _full(w_ck.shape), _full(w_cv.shape)],
        out_specs=(oblk, oblk),
        out_shape=(out, out),
        compiler_params=pltpu.CompilerParams(dimension_semantics=("arbitrary",), vmem_limit_bytes=V7X_VMEM_LIMIT),
        name="memory_kv",
    )(mem, gain, w_ck, w_cv)


def _mix_kernel(x_ref, att_ref, hout_ref, wo_ref, gpost_ref, gpre_ref, wq_ref, h_ref, q_ref):
    half = att_ref.shape[-1]
    mix = (jnp.dot(_bf(att_ref[...]), wo_ref[0:half, :], preferred_element_type=F32)
           + jnp.dot(hout_ref[...], wo_ref[half:, :], preferred_element_type=F32))
    h = x_ref[...] + _rms(mix, gpost_ref[...])
    h_ref[...] = h
    q = jnp.dot(_bf(_rms(h, gpre_ref[...])), wq_ref[...], preferred_element_type=F32)
    q_ref[...] = _bf(q * (q.shape[-1] // XA_HEADS) ** -0.5)


def _mix(x, att, hout, w_out, g_post, g_pre_x, w_cq, tm):
    N, D = x.shape
    half = att.shape[1]
    return pl.pallas_call(
        _mix_kernel,
        grid=(N // tm,),
        in_specs=[
            pl.BlockSpec((tm, D), lambda i: (i, 0)),
            pl.BlockSpec((tm, half), lambda i: (i, 0)),
            pl.BlockSpec((tm, half), lambda i: (i, 0)),
            _full(w_out.shape), _full((1, D)), _full((1, D)), _full(w_cq.shape),
        ],
        out_specs=(pl.BlockSpec((tm, D), lambda i: (i, 0)), pl.BlockSpec((tm, w_cq.shape[1]), lambda i: (i, 0))),
        out_shape=(jax.ShapeDtypeStruct((N, D), F32), jax.ShapeDtypeStruct((N, w_cq.shape[1]), BF16)),
        compiler_params=pltpu.CompilerParams(dimension_semantics=("arbitrary",), vmem_limit_bytes=V7X_VMEM_LIMIT),
        name="mix_out",
    )(x, att, hout, w_out, g_post, g_pre_x, w_cq)


def _xattn_kernel(q_ref, mk_ref, mv_ref, o_ref):
    groups = q_ref.shape[0]
    dh = q_ref.shape[-1] // XA_HEADS
    for gi in range(groups):
        outs = []
        for h in range(XA_HEADS):
            s = _mm_nt(q_ref[gi, :, h * dh:(h + 1) * dh], mk_ref[gi, :, h, :])
            p = jnp.exp(s - jnp.max(s, axis=-1, keepdims=True))
            l = jnp.sum(p, axis=-1, keepdims=True)
            outs.append(_mm(p, mv_ref[gi, :, h, :]) / l)
        o_ref[gi] = _bf(jnp.concatenate(outs, axis=1))


def _xattn(q, mk, mv, groups_per_tile, rows_per_tile):
    NG, R, W = q.shape
    mem_blk = (groups_per_tile,) + mk.shape[1:]
    return pl.pallas_call(
        _xattn_kernel,
        grid=(NG // groups_per_tile, R // rows_per_tile),
        in_specs=[
            pl.BlockSpec((groups_per_tile, rows_per_tile, W), lambda g, r: (g, r, 0)),
            pl.BlockSpec(mem_blk, lambda g, r: (g, 0, 0, 0)),
            pl.BlockSpec(mem_blk, lambda g, r: (g, 0, 0, 0)),
        ],
        out_specs=pl.BlockSpec((groups_per_tile, rows_per_tile, W), lambda g, r: (g, r, 0)),
        out_shape=jax.ShapeDtypeStruct((NG, R, W), BF16),
        compiler_params=pltpu.CompilerParams(dimension_semantics=("arbitrary", "arbitrary"),
                                             vmem_limit_bytes=V7X_VMEM_LIMIT),
        name="cross_attn",
    )(q, mk, mv)


def _ffn_kernel(h_ref, o_ref, wco_ref, gpx_ref, gpf_ref, wup_ref, wdn_ref, gpo_ref, y_ref, *, ff_chunk):
    h = h_ref[...] + _rms(jnp.dot(o_ref[...], wco_ref[...], preferred_element_type=F32), gpx_ref[...])
    hn = _bf(_rms(h, gpf_ref[...]))
    acc = jnp.zeros(h.shape, F32)
    for c in range(wup_ref.shape[1] // ff_chunk):
        cs = slice(c * ff_chunk, (c + 1) * ff_chunk)
        up = jnp.maximum(jnp.dot(hn, wup_ref[:, cs], preferred_element_type=F32), 0.0)
        acc = acc + jnp.dot(_bf(up * up), wdn_ref[cs, :], preferred_element_type=F32)
    y_ref[...] = h + _rms(acc, gpo_ref[...])


def _ffn(h, o, w_co, g_post_x, g_pre_ffn, w_up, w_down, g_post_ffn, tm, ff_chunk):
    N, D = h.shape
    tok = pl.BlockSpec((tm, D), lambda i: (i, 0))
    return pl.pallas_call(
        functools.partial(_ffn_kernel, ff_chunk=ff_chunk),
        grid=(N // tm,),
        in_specs=[tok, pl.BlockSpec((tm, o.shape[1]), lambda i: (i, 0)),
                  _full(w_co.shape), _full((1, D)), _full((1, D)), _full(w_up.shape), _full(w_down.shape),
                  _full((1, D))],
        out_specs=tok,
        out_shape=jax.ShapeDtypeStruct((N, D), F32),
        compiler_params=pltpu.CompilerParams(dimension_semantics=("arbitrary",), vmem_limit_bytes=V7X_VMEM_LIMIT),
        name="co_ffn",
    )(h, o, w_co, g_post_x, g_pre_ffn, w_up, w_down, g_post_ffn)


def _tiles(n_prompt_seq):
    ts = MOBA_BLOCK
    tm = min(512, n_prompt_seq)
    return ts, tm


def kernel(x_prompt, x_sample, cache_k, cache_v, state_hgrn, cache_mem_k, cache_mem_v, page_table, mem_prompt, norm_pre_mix, w_in, hgrn_lb, hgrn_norm, w_out, norm_post_mix, norm_mem, norm_pre_x, w_cq, w_ck, w_cv, w_co, norm_post_x, norm_pre_ffn, w_up, w_down, norm_post_ffn):
    B, S, D = x_prompt.shape
    DB, T, _ = x_sample.shape
    depth = w_in.shape[0]
    past_len = page_table.shape[1] * PAGE_SIZE
    assert D == 2 * MIX_HALF and S % MOBA_BLOCK == 0 and cache_k.shape[2] == PAGE_SIZE
    ts, tm = _tiles(S)
    tm_s = min(tm, DB * T)
    seqs_per_tile = min(8, DB)
    xa_w = w_cq.shape[2]

    lb_all = jnp.cumsum(jax.nn.softmax(hgrn_lb.astype(F32), axis=0), axis=0)
    row = lambda a: a.reshape(1, -1)

    yp, ys = x_prompt, x_sample
    kp_l, vp_l, sp_l, mkp_l, mvp_l, ks_l, vs_l, ss_l = [], [], [], [], [], [], [], []
    for l in range(depth):
        w_qk = w_in[l][:, :2 * MIX_HALF]
        wqk_hi = _bf(w_qk)
        wqk_lo = _bf(w_qk - wqk_hi.astype(F32))
        w_rest = _bf(w_in[l][:, 2 * MIX_HALF:])
        lb = row(lb_all[l])
        hgain = row(hgrn_norm[l])
        wo_b, wcq_b, wck_b, wcv_b, wco_b = _bf(w_out[l]), _bf(w_cq[l]), _bf(w_ck[l]), _bf(w_cv[l]), _bf(w_co[l])
        wup_b, wdn_b = _bf(w_up[l]), _bf(w_down[l])

        def trunk_tail(x2d, att2d, hout2d, q_groups, mk, mv, groups_per_tile, rows_per_tile, tile):
            h1, qx = _mix(x2d, att2d, hout2d, wo_b, row(norm_post_mix[l]), row(norm_pre_x[l]), wcq_b, tile)
            ox = _xattn(qx.reshape(q_groups + (xa_w,)), mk, mv, groups_per_tile, rows_per_tile)
            return _ffn(h1, ox.reshape(-1, xa_w), wco_b, row(norm_post_x[l]), row(norm_pre_ffn[l]), wup_b, wdn_b,
                        row(norm_post_ffn[l]), tile, 1024)

        q, k, v, kbf, vt, kmean, hout, s_fin = _proj_prompt(yp, row(norm_pre_mix[l]), wqk_hi, wqk_lo, w_rest, lb, hgain, ts)
        att = _moba_prompt(q, kbf, vt, kmean)
        mkp, mvp = _memory_kv(mem_prompt, row(norm_mem[l]), wck_b, wcv_b)
        yp = trunk_tail(yp.reshape(B * S, D), att.reshape(B * S, MIX_HALF), hout.reshape(B * S, MIX_HALF),
                        (B, S), mkp, mvp, 1, tm, tm).reshape(B, S, D)
        kp_l.append(k)
        vp_l.append(v)
        sp_l.append(s_fin)
        mkp_l.append(mkp)
        mvp_l.append(mvp)

        q, k, v, hout, s_new = _proj_sample(ys, row(norm_pre_mix[l]), wqk_hi, wqk_lo, w_rest, lb, hgain,
                                            state_hgrn[l], past_len, seqs_per_tile)
        att = _moba_sample(q.reshape(DB, T, MIX_HALF), k, v, cache_k, cache_v, page_table, l, past_len,
                           min(16, page_table.shape[1]))
        ys = trunk_tail(ys.reshape(DB * T, D), att.reshape(DB * T, MIX_HALF), hout, (DB, T),
                        cache_mem_k[l], cache_mem_v[l], min(4, DB), T, tm_s).reshape(DB, T, D)
        ks_l.append(k)
        vs_l.append(v)
        ss_l.append(s_new)
    return (yp, ys, jnp.stack(kp_l), jnp.stack(vp_l), jnp.stack(sp_l), jnp.stack(mkp_l), jnp.stack(mvp_l),
            jnp.stack(ks_l), jnp.stack(vs_l), jnp.stack(ss_l))
```

```python
import functools

import jax
import jax.numpy as jnp
from jax import lax
from jax.experimental import pallas as pl
from jax.experimental.pallas import tpu as pltpu

F32 = jnp.float32
BF16 = jnp.bfloat16

NORM_EPS = 1e-6
ROPE_THETA = 10000.0
HEAD_DIM = 128
N_HEADS = 4
MIX_HALF = N_HEADS * HEAD_DIM
MOBA_BLOCK = 256
MOBA_TOPK = 3
PAGE_SIZE = 128
HGRN_CHUNK = 64
HGRN_SUB = 16
XA_HEADS = 4
LOG2_E = 1.4426950408889634
V7X_VMEM_LIMIT = 56 * 1024 * 1024


def _bf(x):
    return x.astype(BF16)


def _mm(a, b):
    return jnp.dot(_bf(a), _bf(b), preferred_element_type=F32)


def _mm_nt(a, b):
    return lax.dot_general(_bf(a), _bf(b), (((1,), (1,)), ((), ())), preferred_element_type=F32)


def _mm_tn(a, b):
    return lax.dot_general(_bf(a), _bf(b), (((0,), (0,)), ((), ())), preferred_element_type=F32)


def _split2(x):
    hi = x.astype(BF16)
    lo = (x - hi.astype(F32)).astype(BF16)
    return hi, lo


def _split3(x):
    a = x.astype(BF16)
    r = x - a.astype(F32)
    b = r.astype(BF16)
    c = (r - b.astype(F32)).astype(BF16)
    return a, b, c


def _mm3(a, b, transpose_b=False):
    a_hi, a_lo = _split2(a)
    b_hi, b_lo = _split2(b)
    dn = (((1,), (1 if transpose_b else 0,)), ((), ()))
    dot = functools.partial(lax.dot_general, dimension_numbers=dn, preferred_element_type=F32)
    return (dot(a_lo, b_hi) + dot(a_hi, b_lo)) + dot(a_hi, b_hi)


def _mm3_nt(a, b):
    return _mm3(a, b, transpose_b=True)


def _rms(x, gain):
    return x * lax.rsqrt(jnp.mean(x * x, axis=-1, keepdims=True) + NORM_EPS) * gain


def _silu(x):
    return x * (1.0 / (1.0 + jnp.exp(-x)))


def _sigmoid(x):
    return 1.0 / (1.0 + jnp.exp(-x))


def _cumsum_rows(g, lmat):
    g1, g2, g3 = _split3(g)
    dot = functools.partial(jnp.dot, preferred_element_type=F32)
    return (dot(lmat, g3) + dot(lmat, g2)) + dot(lmat, g1)


def _chunk_matrices(rows, chunk, sub):
    r = lax.broadcasted_iota(jnp.int32, (rows, rows), 0)
    c = lax.broadcasted_iota(jnp.int32, (rows, rows), 1)
    same_chunk = (r // chunk) == (c // chunk)
    incl = same_chunk & (c <= r)
    start = same_chunk & (c < (r // sub) * sub)
    end = same_chunk & (c < (r // sub + 1) * sub)
    as_bf = lambda m: jnp.where(m, 1.0, 0.0).astype(BF16)
    return as_bf(incl), as_bf(start), as_bf(end)


def _hgrn_chunk(q, k, v, G, Bs, Be, S, sub, value_major):
    c = q.shape[0]
    nsub = c // sub
    qh = q * jnp.exp(G - Bs)
    kd = k * jnp.exp(Bs - G)
    qg = q * jnp.exp(G)
    row = lax.broadcasted_iota(jnp.int32, (c, c), 0)
    col = lax.broadcasted_iota(jnp.int32, (c, c), 1)
    carried = (lambda: _mm_nt(qg, S)) if value_major else (lambda: _mm(qg, S))
    if nsub == 1:
        att = jnp.where(col <= row, _mm_nt(qh, kd), 0.0)
        o = _mm(att, v) + carried()
    else:
        kh = k * jnp.exp(Be - G)
        lhs = [qh]
        z = jnp.zeros((sub, HEAD_DIM), F32)
        rhs_rows = [jnp.concatenate([kd] + [jnp.zeros_like(kd)] * (nsub - 1), axis=1)]
        for j in range(nsub - 1):
            g_end_j = G[(j + 1) * sub - 1:(j + 1) * sub, :]
            lhs.append(q * jnp.exp(jnp.minimum(G - g_end_j, 0.0)))
            parts = [z] * nsub
            parts[j + 1] = kh[j * sub:(j + 1) * sub, :]
            rhs_rows.append(jnp.concatenate(parts, axis=1))
        n_off = (nsub - 1) * sub
        pad = (-(c + n_off)) % HEAD_DIM
        if pad:
            rhs_rows.append(jnp.zeros((pad, nsub * HEAD_DIM), F32))
        R = _mm_nt(jnp.concatenate(lhs, axis=1), jnp.concatenate(rhs_rows, axis=0))
        width = c + n_off + pad
        row2 = lax.broadcasted_iota(jnp.int32, (c, width), 0)
        col2 = lax.broadcasted_iota(jnp.int32, (c, width), 1)
        diag_ok = (col2 < c) & (col2 <= row2) & ((col2 // sub) == (row2 // sub))
        off_ok = (col2 >= c) & (col2 < c + n_off) & (((col2 - c) // sub) < (row2 // sub))
        att2 = jnp.where(diag_ok | off_ok, R, 0.0)
        v_off = v[:n_off, :]
        v_rows = [v, v_off]
        if pad:
            v_rows.append(jnp.zeros((pad, HEAD_DIM), F32))
        if value_major:
            o = _mm(att2, jnp.concatenate(v_rows, axis=0)) + carried()
        else:
            o = _mm(jnp.concatenate([att2, qg], axis=1), jnp.concatenate(v_rows + [S], axis=0))
    g_end = G[c - 1:c, :]
    kk = k * jnp.exp(g_end - G)
    if value_major:
        S_new = jnp.exp(g_end) * S + _mm_tn(v, kk)
    else:
        decay = jnp.broadcast_to(jnp.exp(g_end), (HEAD_DIM, HEAD_DIM)).T
        S_new = decay * S + _mm_tn(kk, v)
    return o, S_new


def _rotary(x, cos, sin):
    return x * cos + pltpu.roll(x, HEAD_DIM // 2, 1) * sin


def _project(x, gain, wqk_hi, wqk_lo, w_rest, cos, sin, lb):
    hn = _rms(x, gain)
    hn_hi, hn_lo = _split2(hn)
    dot = functools.partial(jnp.dot, preferred_element_type=F32)
    qk = (dot(hn_lo, wqk_hi) + dot(hn_hi, wqk_lo)) + dot(hn_hi, wqk_hi)
    rest = dot(hn_hi, w_rest)
    q = jnp.concatenate([_rotary(qk[:, h * HEAD_DIM:(h + 1) * HEAD_DIM], cos, sin) for h in range(N_HEADS)], axis=1)
    k = jnp.concatenate([_rotary(qk[:, MIX_HALF + h * HEAD_DIM:MIX_HALF + (h + 1) * HEAD_DIM], cos, sin)
                         for h in range(N_HEADS)], axis=1)
    v = rest[:, 0:MIX_HALF]
    hq = _silu(rest[:, MIX_HALF:2 * MIX_HALF])
    f = lb + (1.0 - lb) * _sigmoid(rest[:, 2 * MIX_HALF:3 * MIX_HALF])
    hk = 1.0 - f
    g = jnp.log(f)
    hv = rest[:, 3 * MIX_HALF:4 * MIX_HALF]
    hg = rest[:, 4 * MIX_HALF:5 * MIX_HALF]
    return q, k, v, hq, hk, hv, g, hg


def _hgrn_out(o, gain, hg):
    return _rms(o, gain) * _silu(hg)


def _proj_prompt_kernel(x_ref, gain_ref, wqk_hi_ref, wqk_lo_ref, wrest_ref, cos_ref, sin_ref, lb_ref, hgain_ref,
                        q_ref, k_ref, v_ref, kbf_ref, vt_ref, kmean_ref, hout_ref, state_ref, st_ref):
    t = pl.program_id(1)

    @pl.when(t == 0)
    def _():
        st_ref[...] = jnp.zeros_like(st_ref)

    ts = x_ref.shape[1]
    q, k, v, hq, hk, hv, g, hg = _project(x_ref[0], gain_ref[...], wqk_hi_ref[...], wqk_lo_ref[...], wrest_ref[...],
                                          cos_ref[...], sin_ref[...], lb_ref[...])
    q_ref[0] = q
    for h in range(N_HEADS):
        k_ref[0, pl.ds(h, ts, stride=N_HEADS), :] = k[:, h * HEAD_DIM:(h + 1) * HEAD_DIM]
        v_ref[0, pl.ds(h, ts, stride=N_HEADS), :] = v[:, h * HEAD_DIM:(h + 1) * HEAD_DIM]
    kbf_ref[0] = _bf(k)
    n_blk = ts // MOBA_BLOCK
    for n in range(n_blk):
        sl = slice(n * MOBA_BLOCK, (n + 1) * MOBA_BLOCK)
        vt_ref[0, n] = _bf(v[sl, :].T)
        kmean_ref[0, pl.ds(t * n_blk + n, 1), :] = jnp.mean(k[sl, :], axis=0, keepdims=True)

    l_incl, l_start, l_end = _chunk_matrices(ts, HGRN_CHUNK, HGRN_SUB)
    G = _cumsum_rows(g, l_incl)
    Bs = _cumsum_rows(g, l_start)
    Be = _cumsum_rows(g, l_end)
    hgain = hgain_ref[...]
    for h in range(N_HEADS):
        hs = slice(h * HEAD_DIM, (h + 1) * HEAD_DIM)
        S = st_ref[h]
        outs = []
        for ci in range(ts // HGRN_CHUNK):
            rs = slice(ci * HGRN_CHUNK, (ci + 1) * HGRN_CHUNK)
            o, S = _hgrn_chunk(hq[rs, hs], hk[rs, hs], hv[rs, hs], G[rs, hs], Bs[rs, hs], Be[rs, hs], S, HGRN_SUB, True)
            outs.append(o)
        st_ref[h] = S
        hout_ref[0, :, hs] = _bf(_hgrn_out(jnp.concatenate(outs, axis=0), hgain, hg[:, hs]))

    @pl.when(t == pl.num_programs(1) - 1)
    def _():
        for h in range(N_HEADS):
            state_ref[0, h] = st_ref[h].T


def _proj_sample_kernel(x_ref, gain_ref, wqk_hi_ref, wqk_lo_ref, wrest_ref, cos_ref, sin_ref, lb_ref, hgain_ref,
                        s0_ref, q_ref, k_ref, v_ref, hout_ref, state_ref, *, seq_len):
    rows = x_ref.shape[0]
    q, k, v, hq, hk, hv, g, hg = _project(x_ref[...], gain_ref[...], wqk_hi_ref[...], wqk_lo_ref[...], wrest_ref[...],
                                          cos_ref[...], sin_ref[...], lb_ref[...])
    q_ref[...] = q
    for h in range(N_HEADS):
        k_ref[pl.ds(h, rows, stride=N_HEADS), :] = k[:, h * HEAD_DIM:(h + 1) * HEAD_DIM]
        v_ref[pl.ds(h, rows, stride=N_HEADS), :] = v[:, h * HEAD_DIM:(h + 1) * HEAD_DIM]
    l_incl, _, _ = _chunk_matrices(rows, seq_len, seq_len)
    G = _cumsum_rows(g, l_incl)
    zero = jnp.zeros((seq_len, HEAD_DIM), F32)
    hgain = hgain_ref[...]
    for h in range(N_HEADS):
        hs = slice(h * HEAD_DIM, (h + 1) * HEAD_DIM)
        outs = []
        for si in range(rows // seq_len):
            rs = slice(si * seq_len, (si + 1) * seq_len)
            o, S = _hgrn_chunk(hq[rs, hs], hk[rs, hs], hv[rs, hs], G[rs, hs], zero, zero, s0_ref[si, h], seq_len, False)
            state_ref[si, h] = S
            outs.append(o)
        hout_ref[:, hs] = _bf(_hgrn_out(jnp.concatenate(outs, axis=0), hgain, hg[:, hs]))


def _rotary_tables(pos):
    half = HEAD_DIM // 2
    inv_freq = ROPE_THETA ** (-jnp.arange(half, dtype=F32) / half)
    ang = pos.astype(F32)[:, None] * inv_freq[None, :]
    cos = jnp.cos(ang)
    sin = jnp.sin(ang)
    return jnp.concatenate([cos, cos], axis=1), jnp.concatenate([-sin, sin], axis=1)


def _full(shape):
    return pl.BlockSpec(shape, lambda *_: (0,) * len(shape))


def _proj_prompt(x, gain, wqk_hi, wqk_lo, w_rest, lb, hgain, ts):
    B, S, D = x.shape
    nb = S // MOBA_BLOCK
    cos, sin = _rotary_tables(jnp.arange(S))
    out_shape = (
        jax.ShapeDtypeStruct((B, S, MIX_HALF), F32),
        jax.ShapeDtypeStruct((B, S * N_HEADS, HEAD_DIM), F32),
        jax.ShapeDtypeStruct((B, S * N_HEADS, HEAD_DIM), F32),
        jax.ShapeDtypeStruct((B, S, MIX_HALF), BF16),
        jax.ShapeDtypeStruct((B, nb, MIX_HALF, MOBA_BLOCK), BF16),
        jax.ShapeDtypeStruct((B, nb, MIX_HALF), F32),
        jax.ShapeDtypeStruct((B, S, MIX_HALF), BF16),
        jax.ShapeDtypeStruct((B, N_HEADS, HEAD_DIM, HEAD_DIM), F32),
    )
    tok = pl.BlockSpec((1, ts, MIX_HALF), lambda b, t: (b, t, 0))
    tok_heads = pl.BlockSpec((1, ts * N_HEADS, HEAD_DIM), lambda b, t: (b, t, 0))
    return pl.pallas_call(
        _proj_prompt_kernel,
        grid=(B, S // ts),
        in_specs=[
            pl.BlockSpec((1, ts, D), lambda b, t: (b, t, 0)),
            _full((1, D)), _full(wqk_hi.shape), _full(wqk_lo.shape), _full(w_rest.shape),
            pl.BlockSpec((ts, HEAD_DIM), lambda b, t: (t, 0)),
            pl.BlockSpec((ts, HEAD_DIM), lambda b, t: (t, 0)),
            _full((1, MIX_HALF)), _full((1, HEAD_DIM)),
        ],
        out_specs=(
            tok, tok_heads, tok_heads, tok,
            pl.BlockSpec((1, ts // MOBA_BLOCK, MIX_HALF, MOBA_BLOCK), lambda b, t: (b, t, 0, 0)),
            pl.BlockSpec((1, nb, MIX_HALF), lambda b, t: (b, 0, 0)),
            tok,
            pl.BlockSpec((1, N_HEADS, HEAD_DIM, HEAD_DIM), lambda b, t: (b, 0, 0, 0)),
        ),
        out_shape=out_shape,
        scratch_shapes=[pltpu.VMEM((N_HEADS, HEAD_DIM, HEAD_DIM), F32)],
        compiler_params=pltpu.CompilerParams(dimension_semantics=("arbitrary", "arbitrary"),
                                             vmem_limit_bytes=V7X_VMEM_LIMIT),
        name="proj_prompt",
    )(x, gain, wqk_hi, wqk_lo, w_rest, cos, sin, lb, hgain)


def _proj_sample(x, gain, wqk_hi, wqk_lo, w_rest, lb, hgain, state0, past_len, seqs_per_tile):
    DB, T, D = x.shape
    rows = seqs_per_tile * T
    cos, sin = _rotary_tables(past_len + jnp.arange(T))
    cos = jnp.tile(cos, (seqs_per_tile, 1))
    sin = jnp.tile(sin, (seqs_per_tile, 1))
    out_shape = (
        jax.ShapeDtypeStruct((DB * T, MIX_HALF), F32),
        jax.ShapeDtypeStruct((DB * T * N_HEADS, HEAD_DIM), F32),
        jax.ShapeDtypeStruct((DB * T * N_HEADS, HEAD_DIM), F32),
        jax.ShapeDtypeStruct((DB * T, MIX_HALF), BF16),
        jax.ShapeDtypeStruct((DB, N_HEADS, HEAD_DIM, HEAD_DIM), F32),
    )
    tok = pl.BlockSpec((rows, MIX_HALF), lambda i: (i, 0))
    tok_heads = pl.BlockSpec((rows * N_HEADS, HEAD_DIM), lambda i: (i, 0))
    st = pl.BlockSpec((seqs_per_tile, N_HEADS, HEAD_DIM, HEAD_DIM), lambda i: (i, 0, 0, 0))
    return pl.pallas_call(
        functools.partial(_proj_sample_kernel, seq_len=T),
        grid=(DB // seqs_per_tile,),
        in_specs=[
            pl.BlockSpec((rows, D), lambda i: (i, 0)),
            _full((1, D)), _full(wqk_hi.shape), _full(wqk_lo.shape), _full(w_rest.shape),
            _full((rows, HEAD_DIM)), _full((rows, HEAD_DIM)),
            _full((1, MIX_HALF)), _full((1, HEAD_DIM)),
            st,
        ],
        out_specs=(tok, tok_heads, tok_heads, tok, st),
        out_shape=out_shape,
        compiler_params=pltpu.CompilerParams(dimension_semantics=("arbitrary",),
                                             vmem_limit_bytes=V7X_VMEM_LIMIT),
        name="proj_sample",
    )(x.reshape(DB * T, D), gain, wqk_hi, wqk_lo, w_rest, cos, sin, lb, hgain, state0)


def _top_blocks(gate, n_valid, k_top, axis):
    nb = gate.shape[axis]
    blk = lax.broadcasted_iota(jnp.int32, gate.shape, axis)
    neg = jnp.float32(-jnp.inf)
    g = jnp.where(blk < n_valid, gate, neg)
    sel = jnp.zeros(gate.shape, F32)
    for _ in range(k_top):
        m = jnp.max(g, axis=axis, keepdims=True)
        first = jnp.min(jnp.where(g == m, blk, nb), axis=axis, keepdims=True)
        pick = blk == first
        sel = jnp.where(pick & (m > neg), 1.0, sel)
        g = jnp.where(pick, neg, g)
    return sel


def _moba_prompt_kernel(q_ref, kbf_ref, vt_ref, kmean_ref, o_ref, sel_ref, *, k_top, group):
    i = pl.program_id(2)
    q_t = q_ref[0].T
    neg = jnp.float32(-jnp.inf)
    chosen = _top_blocks(_mm3(kmean_ref[0], q_t), i, k_top, 0)
    sel_ref[...] = jnp.where(chosen > 0.0, 0.0, neg)
    qt = _bf(q_t * (HEAD_DIM ** -0.5 * LOG2_E))

    start = pl.multiple_of(i * MOBA_BLOCK, MOBA_BLOCK)
    s = jnp.dot(kbf_ref[0, pl.ds(start, MOBA_BLOCK), :], qt, preferred_element_type=F32)
    krow = lax.broadcasted_iota(jnp.int32, s.shape, 0)
    qcol = lax.broadcasted_iota(jnp.int32, s.shape, 1)
    s = jnp.where(krow <= qcol, s, neg)
    m = jnp.max(s, axis=0, keepdims=True)
    p = jnp.exp2(s - m)
    l = jnp.sum(p, axis=0, keepdims=True)
    acc = jnp.dot(vt_ref[0, i], _bf(p), preferred_element_type=F32)

    def body(g, carry):
        m, l, acc = carry
        st = pl.multiple_of(g * (group * MOBA_BLOCK), group * MOBA_BLOCK)
        s = jnp.dot(kbf_ref[0, pl.ds(st, group * MOBA_BLOCK), :], qt, preferred_element_type=F32)
        parts = []
        m_new = m
        for u in range(group):
            su = s[u * MOBA_BLOCK:(u + 1) * MOBA_BLOCK, :] + sel_ref[pl.ds(g * group + u, 1), :]
            m_new = jnp.maximum(m_new, jnp.max(su, axis=0, keepdims=True))
            parts.append(su)
        alpha = jnp.exp2(m - m_new)
        l = alpha * l
        acc = alpha * acc
        for u in range(group):
            p = jnp.exp2(parts[u] - m_new)
            l = l + jnp.sum(p, axis=0, keepdims=True)
            acc = acc + jnp.dot(vt_ref[0, g * group + u], _bf(p), preferred_element_type=F32)
        return m_new, l, acc

    m, l, acc = lax.fori_loop(0, (i + group - 1) // group, body, (m, l, acc))
    o_ref[0] = _bf((acc / l).T)


def _moba_prompt(q, kbf, vt, kmean):
    B, S, _ = q.shape
    nb = S // MOBA_BLOCK
    k_top = min(MOBA_TOPK, nb - 1)
    group = 4 if nb % 4 == 0 else (2 if nb % 2 == 0 else 1)
    return pl.pallas_call(
        functools.partial(_moba_prompt_kernel, k_top=k_top, group=group),
        grid=(B, N_HEADS, nb),
        in_specs=[
            pl.BlockSpec((1, MOBA_BLOCK, HEAD_DIM), lambda b, h, i: (b, i, h)),
            pl.BlockSpec((1, S, HEAD_DIM), lambda b, h, i: (b, 0, h)),
            pl.BlockSpec((1, nb, HEAD_DIM, MOBA_BLOCK), lambda b, h, i: (b, 0, h, 0)),
            pl.BlockSpec((1, nb, HEAD_DIM), lambda b, h, i: (b, 0, h)),
        ],
        out_specs=pl.BlockSpec((1, MOBA_BLOCK, HEAD_DIM), lambda b, h, i: (b, i, h)),
        out_shape=jax.ShapeDtypeStruct((B, S, MIX_HALF), BF16),
        scratch_shapes=[pltpu.VMEM((nb, MOBA_BLOCK), F32)],
        compiler_params=pltpu.CompilerParams(dimension_semantics=("arbitrary", "arbitrary", "arbitrary"),
                                             vmem_limit_bytes=V7X_VMEM_LIMIT),
        name="moba_prompt",
    )(q, kbf, vt, kmean)


def _heads_to_lanes(ref):
    tokens = ref.shape[0] // N_HEADS
    return jnp.concatenate([ref[pl.ds(h, tokens, stride=N_HEADS), :] for h in range(N_HEADS)], axis=1)


def _moba_sample_scores_kernel(pt_ref, q_ref, kn_ref, vn_ref, expand_ref, *refs, pages_per_step, n_steps, k_top):
    kp_refs = refs[:pages_per_step]
    p_ref, acc0_ref, linv_ref = refs[pages_per_step:pages_per_step + 3]
    qbd_ref, s_ref, ksum_ref = refs[pages_per_step + 3:]
    del pt_ref
    step = pl.program_id(1)
    T = q_ref.shape[1]
    npair = N_HEADS * T
    n_pages = s_ref.shape[0]
    pages_per_blk = MOBA_BLOCK // PAGE_SIZE
    blks_per_step = pages_per_step // pages_per_blk
    neg = jnp.float32(-jnp.inf)

    @pl.when(step == 0)
    def _():
        pair_head = lax.broadcasted_iota(jnp.int32, (npair, MIX_HALF), 0) // T
        lane_head = lax.broadcasted_iota(jnp.int32, (npair, MIX_HALF), 1) // HEAD_DIM
        q4 = jnp.concatenate([q_ref[0]] * N_HEADS, axis=0)
        qbd_ref[...] = jnp.where(pair_head == lane_head, q4, 0.0)

    qs = _bf(qbd_ref[...] * (HEAD_DIM ** -0.5))
    for n in range(blks_per_step):
        ksum = jnp.zeros((1, MIX_HALF), F32)
        for pp in range(pages_per_blk):
            p = n * pages_per_blk + pp
            kp = _heads_to_lanes(kp_refs[p])
            s_ref[step * pages_per_step + p] = _mm_nt(qs, kp)
            ksum = ksum + jnp.sum(kp, axis=0, keepdims=True)
        ksum_ref[pl.ds(step * blks_per_step + n, 1), :] = ksum

    @pl.when(step == n_steps - 1)
    def _():
        n_blocks = ksum_ref.shape[0]
        kmean = ksum_ref[...] * (1.0 / MOBA_BLOCK)
        sel = _top_blocks(_mm3_nt(qbd_ref[...], kmean), n_blocks, k_top, 1)
        chosen = jnp.dot(_bf(sel), expand_ref[...], preferred_element_type=F32)
        s_own = _mm_nt(qs, _heads_to_lanes(kn_ref))
        key_t = lax.broadcasted_iota(jnp.int32, (npair, T), 1)
        qry_t = lax.broadcasted_iota(jnp.int32, (npair, T), 0) % T
        s_own = jnp.where(key_t <= qry_t, s_own, neg)

        def masked(p):
            return jnp.where(chosen[:, p * PAGE_SIZE:(p + 1) * PAGE_SIZE] > 0.5, s_ref[p], neg)

        m_lanes = masked(0)
        for p in range(1, n_pages):
            m_lanes = jnp.maximum(m_lanes, masked(p))
        m = jnp.maximum(jnp.max(m_lanes, axis=1, keepdims=True), jnp.max(s_own, axis=1, keepdims=True))
        l_lanes = jnp.zeros((npair, PAGE_SIZE), F32)
        for p in range(n_pages):
            w = jnp.exp(masked(p) - m)
            l_lanes = l_lanes + w
            p_ref[0, p] = _bf(w)
        w_own = jnp.exp(s_own - m)
        l = jnp.sum(l_lanes, axis=1, keepdims=True) + jnp.sum(w_own, axis=1, keepdims=True)
        acc0_ref[0] = _mm(w_own, _heads_to_lanes(vn_ref))
        linv_ref[0] = jnp.broadcast_to(1.0 / l, (npair, HEAD_DIM))


def _moba_sample_values_kernel(pt_ref, p_ref, acc0_ref, linv_ref, *refs, pages_per_step, n_steps):
    vp_refs = refs[:pages_per_step]
    o_ref = refs[pages_per_step]
    acc_ref = refs[pages_per_step + 1]
    del pt_ref
    step = pl.program_id(1)
    T = o_ref.shape[1]

    @pl.when(step == 0)
    def _():
        acc_ref[...] = acc0_ref[0]

    acc = acc_ref[...]
    for p in range(pages_per_step):
        vp = _bf(_heads_to_lanes(vp_refs[p]))
        acc = acc + jnp.dot(p_ref[0, step * pages_per_step + p], vp, preferred_element_type=F32)
    acc_ref[...] = acc

    @pl.when(step == n_steps - 1)
    def _():
        linv = linv_ref[0]
        o_ref[0] = jnp.concatenate(
            [acc[h * T:(h + 1) * T, h * HEAD_DIM:(h + 1) * HEAD_DIM] * linv[h * T:(h + 1) * T, :] for h in range(N_HEADS)],
            axis=1)


def _moba_sample(q, k_new, v_new, cache_k, cache_v, page_table, layer, past_len, pages_per_step):
    DB, T, _ = q.shape
    n_pages = page_table.shape[1]
    pages_per_blk = MOBA_BLOCK // PAGE_SIZE
    n_full = past_len // MOBA_BLOCK
    assert n_pages == n_full * pages_per_blk, "the current block must hold no cached rows"
    assert n_pages % pages_per_step == 0 and pages_per_step % pages_per_blk == 0
    n_steps = n_pages // pages_per_step
    k_top = min(MOBA_TOPK, n_full)
    npair = N_HEADS * T
    n_keys = n_pages * PAGE_SIZE
    pt = page_table.astype(jnp.int32).reshape(DB * n_pages)
    expand = (jnp.arange(n_keys)[None, :] // MOBA_BLOCK == jnp.arange(n_full)[:, None]).astype(BF16)

    def page_map(p):
        return lambda b, s, pt_ref: (layer * n_pool + pt_ref[b * n_pages + s * pages_per_step + p], 0)

    per_seq = lambda *tail: (lambda b, s, pt_ref: (b,) + tail)
    page = (PAGE_SIZE * N_HEADS, HEAD_DIM)
    n_pool = cache_k.shape[1]
    cache_k = cache_k.reshape(-1, HEAD_DIM)
    cache_v = cache_v.reshape(-1, HEAD_DIM)
    new_tok = pl.BlockSpec((T * N_HEADS, HEAD_DIM), per_seq(0))
    p_spec = pl.BlockSpec((1, n_pages, npair, PAGE_SIZE), per_seq(0, 0, 0))
    acc_spec = pl.BlockSpec((1, npair, MIX_HALF), per_seq(0, 0))
    linv_spec = pl.BlockSpec((1, npair, HEAD_DIM), per_seq(0, 0))
    params = pltpu.CompilerParams(dimension_semantics=("arbitrary", "arbitrary"), vmem_limit_bytes=V7X_VMEM_LIMIT)

    weights, acc0, linv = pl.pallas_call(
        functools.partial(_moba_sample_scores_kernel, pages_per_step=pages_per_step, n_steps=n_steps, k_top=k_top),
        grid_spec=pltpu.PrefetchScalarGridSpec(
            num_scalar_prefetch=1,
            grid=(DB, n_steps),
            in_specs=[pl.BlockSpec((1, T, MIX_HALF), per_seq(0, 0)), new_tok, new_tok,
                      pl.BlockSpec((n_full, n_keys), lambda b, s, pt_ref: (0, 0))]
            + [pl.BlockSpec(page, page_map(p)) for p in range(pages_per_step)],
            out_specs=(p_spec, acc_spec, linv_spec),
            scratch_shapes=[
                pltpu.VMEM((npair, MIX_HALF), F32),
                pltpu.VMEM((n_pages, npair, PAGE_SIZE), F32),
                pltpu.VMEM((n_full, MIX_HALF), F32),
            ],
        ),
        out_shape=(jax.ShapeDtypeStruct((DB, n_pages, npair, PAGE_SIZE), BF16),
                   jax.ShapeDtypeStruct((DB, npair, MIX_HALF), F32),
                   jax.ShapeDtypeStruct((DB, npair, HEAD_DIM), F32)),
        compiler_params=params,
        name="moba_sample_scores",
    )(pt, q, k_new, v_new, expand, *([cache_k] * pages_per_step))

    return pl.pallas_call(
        functools.partial(_moba_sample_values_kernel, pages_per_step=pages_per_step, n_steps=n_steps),
        grid_spec=pltpu.PrefetchScalarGridSpec(
            num_scalar_prefetch=1,
            grid=(DB, n_steps),
            in_specs=[p_spec, acc_spec, linv_spec] + [pl.BlockSpec(page, page_map(p)) for p in range(pages_per_step)],
            out_specs=pl.BlockSpec((1, T, MIX_HALF), per_seq(0, 0)),
            scratch_shapes=[pltpu.VMEM((npair, MIX_HALF), F32)],
        ),
        out_shape=jax.ShapeDtypeStruct((DB, T, MIX_HALF), F32),
        compiler_params=params,
        name="moba_sample_values",
    )(pt, weights, acc0, linv, *([cache_v] * pages_per_step))


def _memory_kv_kernel(mem_ref, gain_ref, wk_ref, wv_ref, k_ref, v_ref):
    mn = _bf(_rms(mem_ref[0], gain_ref[...]))
    k = jnp.dot(mn, wk_ref[...], preferred_element_type=F32)
    v = jnp.dot(mn, wv_ref[...], preferred_element_type=F32)
    dh = k_ref.shape[-1]
    for h in range(XA_HEADS):
        k_ref[0, :, h, :] = k[:, h * dh:(h + 1) * dh]
        v_ref[0, :, h, :] = v[:, h * dh:(h + 1) * dh]


def _memory_kv(mem, gain, w_ck, w_cv):
    B, M, D = mem.shape
    dh = w_ck.shape[1] // XA_HEADS
    blk = pl.BlockSpec((1, M, D), lambda b: (b, 0, 0))
    out = jax.ShapeDtypeStruct((B, M, XA_HEADS, dh), F32)
    oblk = pl.BlockSpec((1, M, XA_HEADS, dh), lambda b: (b, 0, 0, 0))
    return pl.pallas_call(
        _memory_kv_kernel,
        grid=(B,),
        in_specs=[blk, _full((1, D)), _full(w_ck.shape), _full(w_cv.shape)],
        out_specs=(oblk, oblk),
        out_shape=(out, out),
        compiler_params=pltpu.CompilerParams(dimension_semantics=("arbitrary",), vmem_limit_bytes=V7X_VMEM_LIMIT),
        name="memory_kv",
    )(mem, gain, w_ck, w_cv)


def _mix_kernel(x_ref, att_ref, hout_ref, wo_ref, gpost_ref, gpre_ref, wq_ref, h_ref, q_ref):
    half = att_ref.shape[-1]
    mix = (jnp.dot(_bf(att_ref[...]), wo_ref[0:half, :], preferred_element_type=F32)
           + jnp.dot(hout_ref[...], wo_ref[half:, :], preferred_element_type=F32))
    h = x_ref[...] + _rms(mix, gpost_ref[...])
    h_ref[...] = h
    q = jnp.dot(_bf(_rms(h, gpre_ref[...])), wq_ref[...], preferred_element_type=F32)
    q_ref[...] = _bf(q * (q.shape[-1] // XA_HEADS) ** -0.5)


def _mix(x, att, hout, w_out, g_post, g_pre_x, w_cq, tm):
    N, D = x.shape
    half = att.shape[1]
    return pl.pallas_call(
        _mix_kernel,
        grid=(N // tm,),
        in_specs=[
            pl.BlockSpec((tm, D), lambda i: (i, 0)),
            pl.BlockSpec((tm, half), lambda i: (i, 0)),
            pl.BlockSpec((tm, half), lambda i: (i, 0)),
            _full(w_out.shape), _full((1, D)), _full((1, D)), _full(w_cq.shape),
        ],
        out_specs=(pl.BlockSpec((tm, D), lambda i: (i, 0)), pl.BlockSpec((tm, w_cq.shape[1]), lambda i: (i, 0))),
        out_shape=(jax.ShapeDtypeStruct((N, D), F32), jax.ShapeDtypeStruct((N, w_cq.shape[1]), BF16)),
        compiler_params=pltpu.CompilerParams(dimension_semantics=("arbitrary",), vmem_limit_bytes=V7X_VMEM_LIMIT),
        name="mix_out",
    )(x, att, hout, w_out, g_post, g_pre_x, w_cq)


def _xattn_kernel(q_ref, mk_ref, mv_ref, o_ref):
    groups = q_ref.shape[0]
    dh = q_ref.shape[-1] // XA_HEADS
    pieces = dh // HEAD_DIM
    stride = pieces * XA_HEADS
    M = mk_ref.shape[1] // stride

    def head(ref, gi, h):
        return jnp.concatenate([ref[gi, pl.ds(c * XA_HEADS + h, M, stride=stride), :] for c in range(pieces)], axis=1)

    for gi in range(groups):
        outs = []
        for h in range(XA_HEADS):
            s = _mm_nt(q_ref[gi, :, h * dh:(h + 1) * dh], head(mk_ref, gi, h))
            p = jnp.exp(s - jnp.max(s, axis=-1, keepdims=True))
            l = jnp.sum(p, axis=-1, keepdims=True)
            outs.append(_mm(p, head(mv_ref, gi, h)) / l)
        o_ref[gi] = _bf(jnp.concatenate(outs, axis=1))


def _lane_rows(mem):
    NG, M, H, dh = mem.shape
    pieces = dh // HEAD_DIM
    return mem.reshape(NG, M, H, pieces, HEAD_DIM).transpose(0, 1, 3, 2, 4).reshape(NG, M * pieces * H, HEAD_DIM)


def _xattn(q, mk, mv, groups_per_tile, rows_per_tile):
    NG, R, W = q.shape
    mk = _lane_rows(mk)
    mv = _lane_rows(mv)
    mem_blk = (groups_per_tile,) + mk.shape[1:]
    return pl.pallas_call(
        _xattn_kernel,
        grid=(NG // groups_per_tile, R // rows_per_tile),
        in_specs=[
            pl.BlockSpec((groups_per_tile, rows_per_tile, W), lambda g, r: (g, r, 0)),
            pl.BlockSpec(mem_blk, lambda g, r: (g, 0, 0)),
            pl.BlockSpec(mem_blk, lambda g, r: (g, 0, 0)),
        ],
        out_specs=pl.BlockSpec((groups_per_tile, rows_per_tile, W), lambda g, r: (g, r, 0)),
        out_shape=jax.ShapeDtypeStruct((NG, R, W), BF16),
        compiler_params=pltpu.CompilerParams(dimension_semantics=("arbitrary", "arbitrary"),
                                             vmem_limit_bytes=V7X_VMEM_LIMIT),
        name="cross_attn",
    )(q, mk, mv)


def _ffn_kernel(h_ref, o_ref, wco_ref, gpx_ref, gpf_ref, wup_ref, wdn_ref, gpo_ref, y_ref, *, ff_chunk):
    h = h_ref[...] + _rms(jnp.dot(o_ref[...], wco_ref[...], preferred_element_type=F32), gpx_ref[...])
    hn = _bf(_rms(h, gpf_ref[...]))
    acc = jnp.zeros(h.shape, F32)
    for c in range(wup_ref.shape[1] // ff_chunk):
        cs = slice(c * ff_chunk, (c + 1) * ff_chunk)
        up = jnp.maximum(jnp.dot(hn, wup_ref[:, cs], preferred_element_type=F32), 0.0)
        acc = acc + jnp.dot(_bf(up * up), wdn_ref[cs, :], preferred_element_type=F32)
    y_ref[...] = h + _rms(acc, gpo_ref[...])


def _ffn(h, o, w_co, g_post_x, g_pre_ffn, w_up, w_down, g_post_ffn, tm, ff_chunk):
    N, D = h.shape
    tok = pl.BlockSpec((tm, D), lambda i: (i, 0))
    return pl.pallas_call(
        functools.partial(_ffn_kernel, ff_chunk=ff_chunk),
        grid=(N // tm,),
        in_specs=[tok, pl.BlockSpec((tm, o.shape[1]), lambda i: (i, 0)),
                  _full(w_co.shape), _full((1, D)), _full((1, D)), _full(w_up.shape), _full(w_down.shape),
                  _full((1, D))],
        out_specs=tok,
        out_shape=jax.ShapeDtypeStruct((N, D), F32),
        compiler_params=pltpu.CompilerParams(dimension_semantics=("arbitrary",), vmem_limit_bytes=V7X_VMEM_LIMIT),
        name="co_ffn",
    )(h, o, w_co, g_post_x, g_pre_ffn, w_up, w_down, g_post_ffn)


def _tiles(n_prompt_seq):
    ts = MOBA_BLOCK
    tm = min(512, n_prompt_seq)
    return ts, tm


def kernel(x_prompt, x_sample, cache_k, cache_v, state_hgrn, cache_mem_k, cache_mem_v, page_table, mem_prompt, norm_pre_mix, w_in, hgrn_lb, hgrn_norm, w_out, norm_post_mix, norm_mem, norm_pre_x, w_cq, w_ck, w_cv, w_co, norm_post_x, norm_pre_ffn, w_up, w_down, norm_post_ffn):
    B, S, D = x_prompt.shape
    DB, T, _ = x_sample.shape
    depth = w_in.shape[0]
    past_len = page_table.shape[1] * PAGE_SIZE
    assert D == 2 * MIX_HALF and S % MOBA_BLOCK == 0 and cache_k.shape[2] == PAGE_SIZE
    ts, tm = _tiles(S)
    tm_s = min(tm, DB * T)
    seqs_per_tile = min(8, DB)
    xa_w = w_cq.shape[2]

    lb_all = jnp.cumsum(jax.nn.softmax(hgrn_lb.astype(F32), axis=0), axis=0)
    row = lambda a: a.reshape(1, -1)

    yp, ys = x_prompt, x_sample
    kp_l, vp_l, sp_l, mkp_l, mvp_l, ks_l, vs_l, ss_l = [], [], [], [], [], [], [], []
    for l in range(depth):
        w_qk = w_in[l][:, :2 * MIX_HALF]
        wqk_hi = _bf(w_qk)
        wqk_lo = _bf(w_qk - wqk_hi.astype(F32))
        w_rest = _bf(w_in[l][:, 2 * MIX_HALF:])
        lb = row(lb_all[l])
        hgain = row(hgrn_norm[l])
        wo_b, wcq_b, wck_b, wcv_b, wco_b = _bf(w_out[l]), _bf(w_cq[l]), _bf(w_ck[l]), _bf(w_cv[l]), _bf(w_co[l])
        wup_b, wdn_b = _bf(w_up[l]), _bf(w_down[l])

        def trunk_tail(x2d, att2d, hout2d, q_groups, mk, mv, groups_per_tile, rows_per_tile, tile):
            h1, qx = _mix(x2d, att2d, hout2d, wo_b, row(norm_post_mix[l]), row(norm_pre_x[l]), wcq_b, tile)
            ox = _xattn(qx.reshape(q_groups + (xa_w,)), mk, mv, groups_per_tile, rows_per_tile)
            return _ffn(h1, ox.reshape(-1, xa_w), wco_b, row(norm_post_x[l]), row(norm_pre_ffn[l]), wup_b, wdn_b,
                        row(norm_post_ffn[l]), tile, 1024)

        q, k, v, kbf, vt, kmean, hout, s_fin = _proj_prompt(yp, row(norm_pre_mix[l]), wqk_hi, wqk_lo, w_rest, lb, hgain, ts)
        att = _moba_prompt(q, kbf, vt, kmean)
        mkp, mvp = _memory_kv(mem_prompt, row(norm_mem[l]), wck_b, wcv_b)
        yp = trunk_tail(yp.reshape(B * S, D), att.reshape(B * S, MIX_HALF), hout.reshape(B * S, MIX_HALF),
                        (B, S), mkp, mvp, 1, tm, tm).reshape(B, S, D)
        kp_l.append(k.reshape(B, S, N_HEADS, HEAD_DIM))
        vp_l.append(v.reshape(B, S, N_HEADS, HEAD_DIM))
        sp_l.append(s_fin)
        mkp_l.append(mkp)
        mvp_l.append(mvp)

        q, k, v, hout, s_new = _proj_sample(ys, row(norm_pre_mix[l]), wqk_hi, wqk_lo, w_rest, lb, hgain,
                                            state_hgrn[l], past_len, seqs_per_tile)
        att = _moba_sample(q.reshape(DB, T, MIX_HALF), k, v, cache_k, cache_v, page_table, l, past_len,
                           min(16, page_table.shape[1]))
        ys = trunk_tail(ys.reshape(DB * T, D), att.reshape(DB * T, MIX_HALF), hout, (DB, T),
                        cache_mem_k[l], cache_mem_v[l], min(4, DB), T, tm_s).reshape(DB, T, D)
        ks_l.append(k.reshape(DB, T, N_HEADS, HEAD_DIM))
        vs_l.append(v.reshape(DB, T, N_HEADS, HEAD_DIM))
        ss_l.append(s_new)
    return (yp, ys, jnp.stack(kp_l), jnp.stack(vp_l), jnp.stack(sp_l), jnp.stack(mkp_l), jnp.stack(mvp_l),
            jnp.stack(ks_l), jnp.stack(vs_l), jnp.stack(ss_l))
```

```python
import functools

import jax
import jax.numpy as jnp
from jax import lax
from jax.experimental import pallas as pl
from jax.experimental.pallas import tpu as pltpu

F32 = jnp.float32
BF16 = jnp.bfloat16

NORM_EPS = 1e-6
ROPE_THETA = 10000.0
HEAD_DIM = 128
N_HEADS = 4
MIX_HALF = N_HEADS * HEAD_DIM
MOBA_BLOCK = 256
MOBA_TOPK = 3
PAGE_SIZE = 128
HGRN_CHUNK = 64
HGRN_SUB = 16
XA_HEADS = 4
LOG2_E = 1.4426950408889634
V7X_VMEM_LIMIT = 56 * 1024 * 1024


def _bf(x):
    return x.astype(BF16)


def _mm(a, b):
    return jnp.dot(_bf(a), _bf(b), preferred_element_type=F32)


def _mm_nt(a, b):
    return lax.dot_general(_bf(a), _bf(b), (((1,), (1,)), ((), ())), preferred_element_type=F32)


def _mm_tn(a, b):
    return lax.dot_general(_bf(a), _bf(b), (((0,), (0,)), ((), ())), preferred_element_type=F32)


def _split2(x):
    hi = x.astype(BF16)
    lo = (x - hi.astype(F32)).astype(BF16)
    return hi, lo


def _split3(x):
    a = x.astype(BF16)
    r = x - a.astype(F32)
    b = r.astype(BF16)
    c = (r - b.astype(F32)).astype(BF16)
    return a, b, c


def _mm3(a, b, transpose_b=False):
    a_hi, a_lo = _split2(a)
    b_hi, b_lo = _split2(b)
    dn = (((1,), (1 if transpose_b else 0,)), ((), ()))
    dot = functools.partial(lax.dot_general, dimension_numbers=dn, preferred_element_type=F32)
    return (dot(a_lo, b_hi) + dot(a_hi, b_lo)) + dot(a_hi, b_hi)


def _mm3_nt(a, b):
    return _mm3(a, b, transpose_b=True)


def _rms(x, gain):
    return x * lax.rsqrt(jnp.mean(x * x, axis=-1, keepdims=True) + NORM_EPS) * gain


def _silu(x):
    return x * (1.0 / (1.0 + jnp.exp(-x)))


def _sigmoid(x):
    return 1.0 / (1.0 + jnp.exp(-x))


def _cumsum_rows(g, lmat):
    g1, g2, g3 = _split3(g)
    dot = functools.partial(jnp.dot, preferred_element_type=F32)
    return (dot(lmat, g3) + dot(lmat, g2)) + dot(lmat, g1)


def _chunk_matrices(rows, chunk, sub):
    r = lax.broadcasted_iota(jnp.int32, (rows, rows), 0)
    c = lax.broadcasted_iota(jnp.int32, (rows, rows), 1)
    same_chunk = (r // chunk) == (c // chunk)
    incl = same_chunk & (c <= r)
    start = same_chunk & (c < (r // sub) * sub)
    end = same_chunk & (c < (r // sub + 1) * sub)
    as_bf = lambda m: jnp.where(m, 1.0, 0.0).astype(BF16)
    return as_bf(incl), as_bf(start), as_bf(end)


def _hgrn_chunk(q, k, v, G, Bs, Be, S, sub, value_major):
    c = q.shape[0]
    nsub = c // sub
    qh = q * jnp.exp(G - Bs)
    kd = k * jnp.exp(Bs - G)
    qg = q * jnp.exp(G)
    row = lax.broadcasted_iota(jnp.int32, (c, c), 0)
    col = lax.broadcasted_iota(jnp.int32, (c, c), 1)
    carried = (lambda: _mm_nt(qg, S)) if value_major else (lambda: _mm(qg, S))
    if nsub == 1:
        att = jnp.where(col <= row, _mm_nt(qh, kd), 0.0)
        o = _mm(att, v) + carried()
    else:
        kh = k * jnp.exp(Be - G)
        lhs = [qh]
        z = jnp.zeros((sub, HEAD_DIM), F32)
        rhs_rows = [jnp.concatenate([kd] + [jnp.zeros_like(kd)] * (nsub - 1), axis=1)]
        for j in range(nsub - 1):
            g_end_j = G[(j + 1) * sub - 1:(j + 1) * sub, :]
            lhs.append(q * jnp.exp(jnp.minimum(G - g_end_j, 0.0)))
            parts = [z] * nsub
            parts[j + 1] = kh[j * sub:(j + 1) * sub, :]
            rhs_rows.append(jnp.concatenate(parts, axis=1))
        n_off = (nsub - 1) * sub
        pad = (-(c + n_off)) % HEAD_DIM
        if pad:
            rhs_rows.append(jnp.zeros((pad, nsub * HEAD_DIM), F32))
        R = _mm_nt(jnp.concatenate(lhs, axis=1), jnp.concatenate(rhs_rows, axis=0))
        width = c + n_off + pad
        row2 = lax.broadcasted_iota(jnp.int32, (c, width), 0)
        col2 = lax.broadcasted_iota(jnp.int32, (c, width), 1)
        diag_ok = (col2 < c) & (col2 <= row2) & ((col2 // sub) == (row2 // sub))
        off_ok = (col2 >= c) & (col2 < c + n_off) & (((col2 - c) // sub) < (row2 // sub))
        att2 = jnp.where(diag_ok | off_ok, R, 0.0)
        v_off = v[:n_off, :]
        v_rows = [v, v_off]
        if pad:
            v_rows.append(jnp.zeros((pad, HEAD_DIM), F32))
        if value_major:
            o = _mm(att2, jnp.concatenate(v_rows, axis=0)) + carried()
        else:
            o = _mm(jnp.concatenate([att2, qg], axis=1), jnp.concatenate(v_rows + [S], axis=0))
    g_end = G[c - 1:c, :]
    kk = k * jnp.exp(g_end - G)
    if value_major:
        S_new = jnp.exp(g_end) * S + _mm_tn(v, kk)
    else:
        decay = jnp.broadcast_to(jnp.exp(g_end), (HEAD_DIM, HEAD_DIM)).T
        S_new = decay * S + _mm_tn(kk, v)
    return o, S_new


def _rotary(x, cos, sin):
    return x * cos + pltpu.roll(x, HEAD_DIM // 2, 1) * sin


def _project(x, gain, wqk_hi, wqk_lo, w_rest, cos, sin, lb):
    hn = _rms(x, gain)
    hn_hi, hn_lo = _split2(hn)
    dot = functools.partial(jnp.dot, preferred_element_type=F32)
    qk = (dot(hn_lo, wqk_hi) + dot(hn_hi, wqk_lo)) + dot(hn_hi, wqk_hi)
    rest = dot(hn_hi, w_rest)
    q = jnp.concatenate([_rotary(qk[:, h * HEAD_DIM:(h + 1) * HEAD_DIM], cos, sin) for h in range(N_HEADS)], axis=1)
    k = jnp.concatenate([_rotary(qk[:, MIX_HALF + h * HEAD_DIM:MIX_HALF + (h + 1) * HEAD_DIM], cos, sin)
                         for h in range(N_HEADS)], axis=1)
    v = rest[:, 0:MIX_HALF]
    hq = _silu(rest[:, MIX_HALF:2 * MIX_HALF])
    f = lb + (1.0 - lb) * _sigmoid(rest[:, 2 * MIX_HALF:3 * MIX_HALF])
    hk = 1.0 - f
    g = jnp.log(f)
    hv = rest[:, 3 * MIX_HALF:4 * MIX_HALF]
    hg = rest[:, 4 * MIX_HALF:5 * MIX_HALF]
    return q, k, v, hq, hk, hv, g, hg


def _hgrn_out(o, gain, hg):
    return _rms(o, gain) * _silu(hg)


def _proj_prompt_kernel(x_ref, gain_ref, wqk_hi_ref, wqk_lo_ref, wrest_ref, cos_ref, sin_ref, lb_ref, hgain_ref,
                        q_ref, k_ref, v_ref, kbf_ref, vt_ref, kmean_ref, hout_ref, state_ref, st_ref):
    t = pl.program_id(1)

    @pl.when(t == 0)
    def _():
        st_ref[...] = jnp.zeros_like(st_ref)

    ts = x_ref.shape[1]
    q, k, v, hq, hk, hv, g, hg = _project(x_ref[0], gain_ref[...], wqk_hi_ref[...], wqk_lo_ref[...], wrest_ref[...],
                                          cos_ref[...], sin_ref[...], lb_ref[...])
    q_ref[0] = q
    for h in range(N_HEADS):
        k_ref[0, pl.ds(h, ts, stride=N_HEADS), :] = k[:, h * HEAD_DIM:(h + 1) * HEAD_DIM]
        v_ref[0, pl.ds(h, ts, stride=N_HEADS), :] = v[:, h * HEAD_DIM:(h + 1) * HEAD_DIM]
    kbf_ref[0] = _bf(k)
    n_blk = ts // MOBA_BLOCK
    for n in range(n_blk):
        sl = slice(n * MOBA_BLOCK, (n + 1) * MOBA_BLOCK)
        vt_ref[0, n] = _bf(v[sl, :].T)
        kmean_ref[0, pl.ds(t * n_blk + n, 1), :] = jnp.mean(k[sl, :], axis=0, keepdims=True)

    l_incl, l_start, l_end = _chunk_matrices(ts, HGRN_CHUNK, HGRN_SUB)
    G = _cumsum_rows(g, l_incl)
    Bs = _cumsum_rows(g, l_start)
    Be = _cumsum_rows(g, l_end)
    hgain = hgain_ref[...]
    heads = [slice(h * HEAD_DIM, (h + 1) * HEAD_DIM) for h in range(N_HEADS)]
    states = [st_ref[h] for h in range(N_HEADS)]
    outs = [[] for _ in range(N_HEADS)]
    for ci in range(ts // HGRN_CHUNK):
        rs = slice(ci * HGRN_CHUNK, (ci + 1) * HGRN_CHUNK)
        for h, hs in enumerate(heads):
            o, states[h] = _hgrn_chunk(hq[rs, hs], hk[rs, hs], hv[rs, hs], G[rs, hs], Bs[rs, hs], Be[rs, hs],
                                       states[h], HGRN_SUB, True)
            outs[h].append(o)
    for h, hs in enumerate(heads):
        st_ref[h] = states[h]
        hout_ref[0, :, hs] = _bf(_hgrn_out(jnp.concatenate(outs[h], axis=0), hgain, hg[:, hs]))

    @pl.when(t == pl.num_programs(1) - 1)
    def _():
        for h in range(N_HEADS):
            state_ref[0, h] = st_ref[h].T


def _proj_sample_kernel(x_ref, gain_ref, wqk_hi_ref, wqk_lo_ref, wrest_ref, cos_ref, sin_ref, lb_ref, hgain_ref,
                        s0_ref, q_ref, k_ref, v_ref, hout_ref, state_ref, *, seq_len):
    rows = x_ref.shape[0]
    q, k, v, hq, hk, hv, g, hg = _project(x_ref[...], gain_ref[...], wqk_hi_ref[...], wqk_lo_ref[...], wrest_ref[...],
                                          cos_ref[...], sin_ref[...], lb_ref[...])
    q_ref[...] = q
    for h in range(N_HEADS):
        k_ref[pl.ds(h, rows, stride=N_HEADS), :] = k[:, h * HEAD_DIM:(h + 1) * HEAD_DIM]
        v_ref[pl.ds(h, rows, stride=N_HEADS), :] = v[:, h * HEAD_DIM:(h + 1) * HEAD_DIM]
    l_incl, _, _ = _chunk_matrices(rows, seq_len, seq_len)
    G = _cumsum_rows(g, l_incl)
    zero = jnp.zeros((seq_len, HEAD_DIM), F32)
    hgain = hgain_ref[...]
    for h in range(N_HEADS):
        hs = slice(h * HEAD_DIM, (h + 1) * HEAD_DIM)
        outs = []
        for si in range(rows // seq_len):
            rs = slice(si * seq_len, (si + 1) * seq_len)
            o, S = _hgrn_chunk(hq[rs, hs], hk[rs, hs], hv[rs, hs], G[rs, hs], zero, zero, s0_ref[si, h], seq_len, False)
            state_ref[si, h] = S
            outs.append(o)
        hout_ref[:, hs] = _bf(_hgrn_out(jnp.concatenate(outs, axis=0), hgain, hg[:, hs]))


def _rotary_tables(pos):
    half = HEAD_DIM // 2
    inv_freq = ROPE_THETA ** (-jnp.arange(half, dtype=F32) / half)
    ang = pos.astype(F32)[:, None] * inv_freq[None, :]
    cos = jnp.cos(ang)
    sin = jnp.sin(ang)
    return jnp.concatenate([cos, cos], axis=1), jnp.concatenate([-sin, sin], axis=1)


def _full(shape):
    return pl.BlockSpec(shape, lambda *_: (0,) * len(shape))


def _proj_prompt(x, gain, wqk_hi, wqk_lo, w_rest, lb, hgain, ts):
    B, S, D = x.shape
    nb = S // MOBA_BLOCK
    cos, sin = _rotary_tables(jnp.arange(S))
    out_shape = (
        jax.ShapeDtypeStruct((B, S, MIX_HALF), F32),
        jax.ShapeDtypeStruct((B, S * N_HEADS, HEAD_DIM), F32),
        jax.ShapeDtypeStruct((B, S * N_HEADS, HEAD_DIM), F32),
        jax.ShapeDtypeStruct((B, S, MIX_HALF), BF16),
        jax.ShapeDtypeStruct((B, nb, MIX_HALF, MOBA_BLOCK), BF16),
        jax.ShapeDtypeStruct((B, nb, MIX_HALF), F32),
        jax.ShapeDtypeStruct((B, S, MIX_HALF), BF16),
        jax.ShapeDtypeStruct((B, N_HEADS, HEAD_DIM, HEAD_DIM), F32),
    )
    tok = pl.BlockSpec((1, ts, MIX_HALF), lambda b, t: (b, t, 0))
    tok_heads = pl.BlockSpec((1, ts * N_HEADS, HEAD_DIM), lambda b, t: (b, t, 0))
    return pl.pallas_call(
        _proj_prompt_kernel,
        grid=(B, S // ts),
        in_specs=[
            pl.BlockSpec((1, ts, D), lambda b, t: (b, t, 0)),
            _full((1, D)), _full(wqk_hi.shape), _full(wqk_lo.shape), _full(w_rest.shape),
            pl.BlockSpec((ts, HEAD_DIM), lambda b, t: (t, 0)),
            pl.BlockSpec((ts, HEAD_DIM), lambda b, t: (t, 0)),
            _full((1, MIX_HALF)), _full((1, HEAD_DIM)),
        ],
        out_specs=(
            tok, tok_heads, tok_heads, tok,
            pl.BlockSpec((1, ts // MOBA_BLOCK, MIX_HALF, MOBA_BLOCK), lambda b, t: (b, t, 0, 0)),
            pl.BlockSpec((1, nb, MIX_HALF), lambda b, t: (b, 0, 0)),
            tok,
            pl.BlockSpec((1, N_HEADS, HEAD_DIM, HEAD_DIM), lambda b, t: (b, 0, 0, 0)),
        ),
        out_shape=out_shape,
        scratch_shapes=[pltpu.VMEM((N_HEADS, HEAD_DIM, HEAD_DIM), F32)],
        compiler_params=pltpu.CompilerParams(dimension_semantics=("arbitrary", "arbitrary"),
                                             vmem_limit_bytes=V7X_VMEM_LIMIT),
        name="proj_prompt",
    )(x, gain, wqk_hi, wqk_lo, w_rest, cos, sin, lb, hgain)


def _proj_sample(x, gain, wqk_hi, wqk_lo, w_rest, lb, hgain, state0, past_len, seqs_per_tile):
    DB, T, D = x.shape
    rows = seqs_per_tile * T
    cos, sin = _rotary_tables(past_len + jnp.arange(T))
    cos = jnp.tile(cos, (seqs_per_tile, 1))
    sin = jnp.tile(sin, (seqs_per_tile, 1))
    out_shape = (
        jax.ShapeDtypeStruct((DB * T, MIX_HALF), F32),
        jax.ShapeDtypeStruct((DB * T * N_HEADS, HEAD_DIM), F32),
        jax.ShapeDtypeStruct((DB * T * N_HEADS, HEAD_DIM), F32),
        jax.ShapeDtypeStruct((DB * T, MIX_HALF), BF16),
        jax.ShapeDtypeStruct((DB, N_HEADS, HEAD_DIM, HEAD_DIM), F32),
    )
    tok = pl.BlockSpec((rows, MIX_HALF), lambda i: (i, 0))
    tok_heads = pl.BlockSpec((rows * N_HEADS, HEAD_DIM), lambda i: (i, 0))
    st = pl.BlockSpec((seqs_per_tile, N_HEADS, HEAD_DIM, HEAD_DIM), lambda i: (i, 0, 0, 0))
    return pl.pallas_call(
        functools.partial(_proj_sample_kernel, seq_len=T),
        grid=(DB // seqs_per_tile,),
        in_specs=[
            pl.BlockSpec((rows, D), lambda i: (i, 0)),
            _full((1, D)), _full(wqk_hi.shape), _full(wqk_lo.shape), _full(w_rest.shape),
            _full((rows, HEAD_DIM)), _full((rows, HEAD_DIM)),
            _full((1, MIX_HALF)), _full((1, HEAD_DIM)),
            st,
        ],
        out_specs=(tok, tok_heads, tok_heads, tok, st),
        out_shape=out_shape,
        compiler_params=pltpu.CompilerParams(dimension_semantics=("arbitrary",),
                                             vmem_limit_bytes=V7X_VMEM_LIMIT),
        name="proj_sample",
    )(x.reshape(DB * T, D), gain, wqk_hi, wqk_lo, w_rest, cos, sin, lb, hgain, state0)


def _top_blocks(gate, n_valid, k_top, axis):
    nb = gate.shape[axis]
    blk = lax.broadcasted_iota(jnp.int32, gate.shape, axis)
    neg = jnp.float32(-jnp.inf)
    g = jnp.where(blk < n_valid, gate, neg)
    sel = jnp.zeros(gate.shape, F32)
    for _ in range(k_top):
        m = jnp.max(g, axis=axis, keepdims=True)
        first = jnp.min(jnp.where(g == m, blk, nb), axis=axis, keepdims=True)
        pick = blk == first
        sel = jnp.where(pick & (m > neg), 1.0, sel)
        g = jnp.where(pick, neg, g)
    return sel


def _moba_prompt_kernel(q_ref, kbf_ref, vt_ref, kmean_ref, o_ref, sel_ref, sa_ref, sb_ref, *, k_top, group):
    i = pl.program_id(2)
    q_t = q_ref[0].T
    neg = jnp.float32(-jnp.inf)
    chosen = _top_blocks(_mm3(kmean_ref[0], q_t), i, k_top, 0)
    sel_ref[...] = jnp.where(chosen > 0.0, 0.0, neg)
    qt = _bf(q_t * (HEAD_DIM ** -0.5 * LOG2_E))

    start = pl.multiple_of(i * MOBA_BLOCK, MOBA_BLOCK)
    s = jnp.dot(kbf_ref[0, pl.ds(start, MOBA_BLOCK), :], qt, preferred_element_type=F32)
    krow = lax.broadcasted_iota(jnp.int32, s.shape, 0)
    qcol = lax.broadcasted_iota(jnp.int32, s.shape, 1)
    s = jnp.where(krow <= qcol, s, neg)
    m = jnp.max(s, axis=0, keepdims=True)
    p = jnp.exp2(s - m)
    l = jnp.sum(p, axis=0, keepdims=True)
    acc = jnp.dot(vt_ref[0, i], _bf(p), preferred_element_type=F32)

    rows = group * MOBA_BLOCK
    last_group = kbf_ref.shape[1] // rows - 1

    def score(g, buf):
        g = jnp.minimum(g, last_group)
        st = pl.multiple_of(g * rows, rows)
        s = jnp.dot(kbf_ref[0, pl.ds(st, rows), :], qt, preferred_element_type=F32)
        mg = None
        for u in range(group):
            su = s[u * MOBA_BLOCK:(u + 1) * MOBA_BLOCK, :] + sel_ref[pl.ds(g * group + u, 1), :]
            buf[u * MOBA_BLOCK:(u + 1) * MOBA_BLOCK, :] = su
            mu = jnp.max(su, axis=0, keepdims=True)
            mg = mu if mg is None else jnp.maximum(mg, mu)
        return mg

    def absorb(g, buf, mg, m, l, acc):
        m_new = jnp.maximum(m, mg)
        alpha = jnp.exp2(m - m_new)
        l = alpha * l
        acc = alpha * acc
        for u in range(group):
            p = jnp.exp2(buf[u * MOBA_BLOCK:(u + 1) * MOBA_BLOCK, :] - m_new)
            l = l + jnp.sum(p, axis=0, keepdims=True)
            acc = acc + jnp.dot(vt_ref[0, g * group + u], _bf(p), preferred_element_type=F32)
        return m_new, l, acc

    def body(k, carry):
        m, l, acc, m_a = carry
        m_b = score(2 * k + 1, sb_ref)
        m, l, acc = absorb(2 * k, sa_ref, m_a, m, l, acc)
        m_a = score(2 * k + 2, sa_ref)
        m, l, acc = absorb(2 * k + 1, sb_ref, m_b, m, l, acc)
        return m, l, acc, m_a

    n_groups = (i + group - 1) // group
    m, l, acc, _ = lax.fori_loop(0, (n_groups + 1) // 2, body, (m, l, acc, score(0, sa_ref)))
    o_ref[0] = _bf((acc / l).T)


def _moba_prompt(q, kbf, vt, kmean):
    B, S, _ = q.shape
    nb = S // MOBA_BLOCK
    k_top = min(MOBA_TOPK, nb - 1)
    group = 2 if nb % 4 == 0 else 1
    assert nb % (2 * group) == 0, "past blocks are absorbed two groups per loop trip"
    return pl.pallas_call(
        functools.partial(_moba_prompt_kernel, k_top=k_top, group=group),
        grid=(B, N_HEADS, nb),
        in_specs=[
            pl.BlockSpec((1, MOBA_BLOCK, HEAD_DIM), lambda b, h, i: (b, i, h)),
            pl.BlockSpec((1, S, HEAD_DIM), lambda b, h, i: (b, 0, h)),
            pl.BlockSpec((1, nb, HEAD_DIM, MOBA_BLOCK), lambda b, h, i: (b, 0, h, 0)),
            pl.BlockSpec((1, nb, HEAD_DIM), lambda b, h, i: (b, 0, h)),
        ],
        out_specs=pl.BlockSpec((1, MOBA_BLOCK, HEAD_DIM), lambda b, h, i: (b, i, h)),
        out_shape=jax.ShapeDtypeStruct((B, S, MIX_HALF), BF16),
        scratch_shapes=[pltpu.VMEM((nb, MOBA_BLOCK), F32),
                        pltpu.VMEM((group * MOBA_BLOCK, MOBA_BLOCK), F32),
                        pltpu.VMEM((group * MOBA_BLOCK, MOBA_BLOCK), F32)],
        compiler_params=pltpu.CompilerParams(dimension_semantics=("arbitrary", "arbitrary", "arbitrary"),
                                             vmem_limit_bytes=V7X_VMEM_LIMIT),
        name="moba_prompt",
    )(q, kbf, vt, kmean)


def _heads_to_lanes(ref):
    tokens = ref.shape[0] // N_HEADS
    return jnp.concatenate([ref[pl.ds(h, tokens, stride=N_HEADS), :] for h in range(N_HEADS)], axis=1)


def _moba_sample_scores_kernel(pt_ref, q_ref, kn_ref, vn_ref, expand_ref, *refs, pages_per_step, n_steps, k_top):
    kp_refs = refs[:pages_per_step]
    p_ref, acc0_ref, linv_ref = refs[pages_per_step:pages_per_step + 3]
    qbd_ref, s_ref, ksum_ref = refs[pages_per_step + 3:]
    del pt_ref
    step = pl.program_id(1)
    T = q_ref.shape[1]
    npair = N_HEADS * T
    n_pages = s_ref.shape[0]
    pages_per_blk = MOBA_BLOCK // PAGE_SIZE
    blks_per_step = pages_per_step // pages_per_blk
    neg = jnp.float32(-jnp.inf)

    @pl.when(step == 0)
    def _():
        pair_head = lax.broadcasted_iota(jnp.int32, (npair, MIX_HALF), 0) // T
        lane_head = lax.broadcasted_iota(jnp.int32, (npair, MIX_HALF), 1) // HEAD_DIM
        q4 = jnp.concatenate([q_ref[0]] * N_HEADS, axis=0)
        qbd_ref[...] = jnp.where(pair_head == lane_head, q4, 0.0)

    qs = _bf(qbd_ref[...] * (HEAD_DIM ** -0.5))
    for n in range(blks_per_step):
        ksum = jnp.zeros((1, MIX_HALF), F32)
        for pp in range(pages_per_blk):
            p = n * pages_per_blk + pp
            kp = _heads_to_lanes(kp_refs[p])
            s_ref[step * pages_per_step + p] = _mm_nt(qs, kp)
            ksum = ksum + jnp.sum(kp, axis=0, keepdims=True)
        ksum_ref[pl.ds(step * blks_per_step + n, 1), :] = ksum

    @pl.when(step == n_steps - 1)
    def _():
        n_blocks = ksum_ref.shape[0]
        kmean = ksum_ref[...] * (1.0 / MOBA_BLOCK)
        sel = _top_blocks(_mm3_nt(kmean, qbd_ref[...]), n_blocks, k_top, 0)
        chosen = _mm_tn(sel, expand_ref[...])
        s_own = _mm_nt(qs, _heads_to_lanes(kn_ref))
        key_t = lax.broadcasted_iota(jnp.int32, (npair, T), 1)
        qry_t = lax.broadcasted_iota(jnp.int32, (npair, T), 0) % T
        s_own = jnp.where(key_t <= qry_t, s_own, neg)

        def masked(p):
            return jnp.where(chosen[:, p * PAGE_SIZE:(p + 1) * PAGE_SIZE] > 0.5, s_ref[p], neg)

        m_lanes = masked(0)
        for p in range(1, n_pages):
            m_lanes = jnp.maximum(m_lanes, masked(p))
        m = jnp.maximum(jnp.max(m_lanes, axis=1, keepdims=True), jnp.max(s_own, axis=1, keepdims=True))
        l_lanes = jnp.zeros((npair, PAGE_SIZE), F32)
        for p in range(n_pages):
            w = jnp.exp(masked(p) - m)
            l_lanes = l_lanes + w
            p_ref[0, p] = _bf(w)
        w_own = jnp.exp(s_own - m)
        l = jnp.sum(l_lanes, axis=1, keepdims=True) + jnp.sum(w_own, axis=1, keepdims=True)
        acc0_ref[0] = _mm(w_own, _heads_to_lanes(vn_ref))
        linv_ref[0] = jnp.broadcast_to(1.0 / l, (npair, HEAD_DIM))


def _moba_sample_values_kernel(pt_ref, p_ref, acc0_ref, linv_ref, *refs, pages_per_step, n_steps):
    vp_refs = refs[:pages_per_step]
    o_ref = refs[pages_per_step]
    acc_ref = refs[pages_per_step + 1]
    del pt_ref
    step = pl.program_id(1)
    T = o_ref.shape[1]

    @pl.when(step == 0)
    def _():
        acc_ref[...] = acc0_ref[0]

    acc = acc_ref[...]
    for p in range(pages_per_step):
        vp = _bf(_heads_to_lanes(vp_refs[p]))
        acc = acc + jnp.dot(p_ref[0, step * pages_per_step + p], vp, preferred_element_type=F32)
    acc_ref[...] = acc

    @pl.when(step == n_steps - 1)
    def _():
        linv = linv_ref[0]
        o_ref[0] = jnp.concatenate(
            [acc[h * T:(h + 1) * T, h * HEAD_DIM:(h + 1) * HEAD_DIM] * linv[h * T:(h + 1) * T, :] for h in range(N_HEADS)],
            axis=1)


def _moba_sample(q, k_new, v_new, cache_k, cache_v, page_table, layer, past_len, pages_per_step):
    DB, T, _ = q.shape
    n_pages = page_table.shape[1]
    pages_per_blk = MOBA_BLOCK // PAGE_SIZE
    n_full = past_len // MOBA_BLOCK
    assert n_pages == n_full * pages_per_blk, "the current block must hold no cached rows"
    assert n_pages % pages_per_step == 0 and pages_per_step % pages_per_blk == 0
    n_steps = n_pages // pages_per_step
    k_top = min(MOBA_TOPK, n_full)
    npair = N_HEADS * T
    n_keys = n_pages * PAGE_SIZE
    pt = page_table.astype(jnp.int32).reshape(DB * n_pages)
    expand = (jnp.arange(n_keys)[None, :] // MOBA_BLOCK == jnp.arange(n_full)[:, None]).astype(BF16)

    def page_map(p):
        return lambda b, s, pt_ref: (layer * n_pool + pt_ref[b * n_pages + s * pages_per_step + p], 0)

    per_seq = lambda *tail: (lambda b, s, pt_ref: (b,) + tail)
    page = (PAGE_SIZE * N_HEADS, HEAD_DIM)
    n_pool = cache_k.shape[1]
    cache_k = cache_k.reshape(-1, HEAD_DIM)
    cache_v = cache_v.reshape(-1, HEAD_DIM)
    new_tok = pl.BlockSpec((T * N_HEADS, HEAD_DIM), per_seq(0))
    p_spec = pl.BlockSpec((1, n_pages, npair, PAGE_SIZE), per_seq(0, 0, 0))
    acc_spec = pl.BlockSpec((1, npair, MIX_HALF), per_seq(0, 0))
    linv_spec = pl.BlockSpec((1, npair, HEAD_DIM), per_seq(0, 0))
    params = pltpu.CompilerParams(dimension_semantics=("arbitrary", "arbitrary"), vmem_limit_bytes=V7X_VMEM_LIMIT)

    weights, acc0, linv = pl.pallas_call(
        functools.partial(_moba_sample_scores_kernel, pages_per_step=pages_per_step, n_steps=n_steps, k_top=k_top),
        grid_spec=pltpu.PrefetchScalarGridSpec(
            num_scalar_prefetch=1,
            grid=(DB, n_steps),
            in_specs=[pl.BlockSpec((1, T, MIX_HALF), per_seq(0, 0)), new_tok, new_tok,
                      pl.BlockSpec((n_full, n_keys), lambda b, s, pt_ref: (0, 0))]
            + [pl.BlockSpec(page, page_map(p)) for p in range(pages_per_step)],
            out_specs=(p_spec, acc_spec, linv_spec),
            scratch_shapes=[
                pltpu.VMEM((npair, MIX_HALF), F32),
                pltpu.VMEM((n_pages, npair, PAGE_SIZE), F32),
                pltpu.VMEM((n_full, MIX_HALF), F32),
            ],
        ),
        out_shape=(jax.ShapeDtypeStruct((DB, n_pages, npair, PAGE_SIZE), BF16),
                   jax.ShapeDtypeStruct((DB, npair, MIX_HALF), F32),
                   jax.ShapeDtypeStruct((DB, npair, HEAD_DIM), F32)),
        compiler_params=params,
        name="moba_sample_scores",
    )(pt, q, k_new, v_new, expand, *([cache_k] * pages_per_step))

    return pl.pallas_call(
        functools.partial(_moba_sample_values_kernel, pages_per_step=pages_per_step, n_steps=n_steps),
        grid_spec=pltpu.PrefetchScalarGridSpec(
            num_scalar_prefetch=1,
            grid=(DB, n_steps),
            in_specs=[p_spec, acc_spec, linv_spec] + [pl.BlockSpec(page, page_map(p)) for p in range(pages_per_step)],
            out_specs=pl.BlockSpec((1, T, MIX_HALF), per_seq(0, 0)),
            scratch_shapes=[pltpu.VMEM((npair, MIX_HALF), F32)],
        ),
        out_shape=jax.ShapeDtypeStruct((DB, T, MIX_HALF), F32),
        compiler_params=params,
        name="moba_sample_values",
    )(pt, weights, acc0, linv, *([cache_v] * pages_per_step))


def _memory_kv_kernel(mem_ref, gain_ref, wk_ref, wv_ref, k_ref, v_ref):
    mn = _bf(_rms(mem_ref[0], gain_ref[...]))
    k = jnp.dot(mn, wk_ref[...], preferred_element_type=F32)
    v = jnp.dot(mn, wv_ref[...], preferred_element_type=F32)
    dh = k_ref.shape[-1]
    for h in range(XA_HEADS):
        k_ref[0, :, h, :] = k[:, h * dh:(h + 1) * dh]
        v_ref[0, :, h, :] = v[:, h * dh:(h + 1) * dh]


def _memory_kv(mem, gain, w_ck, w_cv):
    B, M, D = mem.shape
    dh = w_ck.shape[1] // XA_HEADS
    blk = pl.BlockSpec((1, M, D), lambda b: (b, 0, 0))
    out = jax.ShapeDtypeStruct((B, M, XA_HEADS, dh), F32)
    oblk = pl.BlockSpec((1, M, XA_HEADS, dh), lambda b: (b, 0, 0, 0))
    return pl.pallas_call(
        _memory_kv_kernel,
        grid=(B,),
        in_specs=[blk, _full((1, D)), _full(w_ck.shape), _full(w_cv.shape)],
        out_specs=(oblk, oblk),
        out_shape=(out, out),
        compiler_params=pltpu.CompilerParams(dimension_semantics=("arbitrary",), vmem_limit_bytes=V7X_VMEM_LIMIT),
        name="memory_kv",
    )(mem, gain, w_ck, w_cv)


def _mix_kernel(x_ref, att_ref, hout_ref, wo_ref, gpost_ref, gpre_ref, wq_ref, h_ref, q_ref):
    half = att_ref.shape[-1]
    mix = (jnp.dot(_bf(att_ref[...]), wo_ref[0:half, :], preferred_element_type=F32)
           + jnp.dot(hout_ref[...], wo_ref[half:, :], preferred_element_type=F32))
    h = x_ref[...] + _rms(mix, gpost_ref[...])
    h_ref[...] = h
    q = jnp.dot(_bf(_rms(h, gpre_ref[...])), wq_ref[...], preferred_element_type=F32)
    q_ref[...] = _bf(q * (q.shape[-1] // XA_HEADS) ** -0.5)


def _mix(x, att, hout, w_out, g_post, g_pre_x, w_cq, tm):
    N, D = x.shape
    half = att.shape[1]
    return pl.pallas_call(
        _mix_kernel,
        grid=(N // tm,),
        in_specs=[
            pl.BlockSpec((tm, D), lambda i: (i, 0)),
            pl.BlockSpec((tm, half), lambda i: (i, 0)),
            pl.BlockSpec((tm, half), lambda i: (i, 0)),
            _full(w_out.shape), _full((1, D)), _full((1, D)), _full(w_cq.shape),
        ],
        out_specs=(pl.BlockSpec((tm, D), lambda i: (i, 0)), pl.BlockSpec((tm, w_cq.shape[1]), lambda i: (i, 0))),
        out_shape=(jax.ShapeDtypeStruct((N, D), F32), jax.ShapeDtypeStruct((N, w_cq.shape[1]), BF16)),
        compiler_params=pltpu.CompilerParams(dimension_semantics=("arbitrary",), vmem_limit_bytes=V7X_VMEM_LIMIT),
        name="mix_out",
    )(x, att, hout, w_out, g_post, g_pre_x, w_cq)


def _xattn_kernel(q_ref, mk_ref, mv_ref, o_ref):
    groups = q_ref.shape[0]
    dh = q_ref.shape[-1] // XA_HEADS
    pieces = dh // HEAD_DIM
    stride = pieces * XA_HEADS
    M = mk_ref.shape[1] // stride

    def head(ref, gi, h):
        return jnp.concatenate([ref[gi, pl.ds(c * XA_HEADS + h, M, stride=stride), :] for c in range(pieces)], axis=1)

    for gi in range(groups):
        outs = []
        for h in range(XA_HEADS):
            s = _mm_nt(q_ref[gi, :, h * dh:(h + 1) * dh], head(mk_ref, gi, h))
            p = jnp.exp(s - jnp.max(s, axis=-1, keepdims=True))
            l = jnp.sum(p, axis=-1, keepdims=True)
            outs.append(_mm(p, head(mv_ref, gi, h)) / l)
        o_ref[gi] = _bf(jnp.concatenate(outs, axis=1))


def _lane_rows(mem):
    NG, M, H, dh = mem.shape
    pieces = dh // HEAD_DIM
    return mem.reshape(NG, M, H, pieces, HEAD_DIM).transpose(0, 1, 3, 2, 4).reshape(NG, M * pieces * H, HEAD_DIM)


def _xattn(q, mk, mv, groups_per_tile, rows_per_tile):
    NG, R, W = q.shape
    mk = _lane_rows(mk)
    mv = _lane_rows(mv)
    mem_blk = (groups_per_tile,) + mk.shape[1:]
    return pl.pallas_call(
        _xattn_kernel,
        grid=(NG // groups_per_tile, R // rows_per_tile),
        in_specs=[
            pl.BlockSpec((groups_per_tile, rows_per_tile, W), lambda g, r: (g, r, 0)),
            pl.BlockSpec(mem_blk, lambda g, r: (g, 0, 0)),
            pl.BlockSpec(mem_blk, lambda g, r: (g, 0, 0)),
        ],
        out_specs=pl.BlockSpec((groups_per_tile, rows_per_tile, W), lambda g, r: (g, r, 0)),
        out_shape=jax.ShapeDtypeStruct((NG, R, W), BF16),
        compiler_params=pltpu.CompilerParams(dimension_semantics=("arbitrary", "arbitrary"),
                                             vmem_limit_bytes=V7X_VMEM_LIMIT),
        name="cross_attn",
    )(q, mk, mv)


def _ffn_kernel(h_ref, o_ref, wco_ref, gpx_ref, gpf_ref, wup_ref, wdn_ref, gpo_ref, y_ref, *, ff_chunk):
    h = h_ref[...] + _rms(jnp.dot(o_ref[...], wco_ref[...], preferred_element_type=F32), gpx_ref[...])
    hn = _bf(_rms(h, gpf_ref[...]))
    acc = jnp.zeros(h.shape, F32)
    for c in range(wup_ref.shape[1] // ff_chunk):
        cs = slice(c * ff_chunk, (c + 1) * ff_chunk)
        up = jnp.maximum(jnp.dot(hn, wup_ref[:, cs], preferred_element_type=F32), 0.0)
        acc = acc + jnp.dot(_bf(up * up), wdn_ref[cs, :], preferred_element_type=F32)
    y_ref[...] = h + _rms(acc, gpo_ref[...])


def _ffn(h, o, w_co, g_post_x, g_pre_ffn, w_up, w_down, g_post_ffn, tm, ff_chunk):
    N, D = h.shape
    tok = pl.BlockSpec((tm, D), lambda i: (i, 0))
    return pl.pallas_call(
        functools.partial(_ffn_kernel, ff_chunk=ff_chunk),
        grid=(N // tm,),
        in_specs=[tok, pl.BlockSpec((tm, o.shape[1]), lambda i: (i, 0)),
                  _full(w_co.shape), _full((1, D)), _full((1, D)), _full(w_up.shape), _full(w_down.shape),
                  _full((1, D))],
        out_specs=tok,
        out_shape=jax.ShapeDtypeStruct((N, D), F32),
        compiler_params=pltpu.CompilerParams(dimension_semantics=("arbitrary",), vmem_limit_bytes=V7X_VMEM_LIMIT),
        name="co_ffn",
    )(h, o, w_co, g_post_x, g_pre_ffn, w_up, w_down, g_post_ffn)


def _tiles(n_prompt_seq):
    ts = MOBA_BLOCK
    tm = min(512, n_prompt_seq)
    return ts, tm


def kernel(x_prompt, x_sample, cache_k, cache_v, state_hgrn, cache_mem_k, cache_mem_v, page_table, mem_prompt, norm_pre_mix, w_in, hgrn_lb, hgrn_norm, w_out, norm_post_mix, norm_mem, norm_pre_x, w_cq, w_ck, w_cv, w_co, norm_post_x, norm_pre_ffn, w_up, w_down, norm_post_ffn):
    B, S, D = x_prompt.shape
    DB, T, _ = x_sample.shape
    depth = w_in.shape[0]
    past_len = page_table.shape[1] * PAGE_SIZE
    assert D == 2 * MIX_HALF and S % MOBA_BLOCK == 0 and cache_k.shape[2] == PAGE_SIZE
    ts, tm = _tiles(S)
    tm_s = min(tm, DB * T)
    seqs_per_tile = min(8, DB)
    xa_w = w_cq.shape[2]

    lb_all = jnp.cumsum(jax.nn.softmax(hgrn_lb.astype(F32), axis=0), axis=0)
    row = lambda a: a.reshape(1, -1)

    yp, ys = x_prompt, x_sample
    kp_l, vp_l, sp_l, mkp_l, mvp_l, ks_l, vs_l, ss_l = [], [], [], [], [], [], [], []
    for l in range(depth):
        w_qk = w_in[l][:, :2 * MIX_HALF]
        wqk_hi = _bf(w_qk)
        wqk_lo = _bf(w_qk - wqk_hi.astype(F32))
        w_rest = _bf(w_in[l][:, 2 * MIX_HALF:])
        lb = row(lb_all[l])
        hgain = row(hgrn_norm[l])
        wo_b, wcq_b, wck_b, wcv_b, wco_b = _bf(w_out[l]), _bf(w_cq[l]), _bf(w_ck[l]), _bf(w_cv[l]), _bf(w_co[l])
        wup_b, wdn_b = _bf(w_up[l]), _bf(w_down[l])

        def trunk_tail(x2d, att2d, hout2d, q_groups, mk, mv, groups_per_tile, rows_per_tile, tile):
            h1, qx = _mix(x2d, att2d, hout2d, wo_b, row(norm_post_mix[l]), row(norm_pre_x[l]), wcq_b, tile)
            ox = _xattn(qx.reshape(q_groups + (xa_w,)), mk, mv, groups_per_tile, rows_per_tile)
            return _ffn(h1, ox.reshape(-1, xa_w), wco_b, row(norm_post_x[l]), row(norm_pre_ffn[l]), wup_b, wdn_b,
                        row(norm_post_ffn[l]), tile, 1024)

        q, k, v, kbf, vt, kmean, hout, s_fin = _proj_prompt(yp, row(norm_pre_mix[l]), wqk_hi, wqk_lo, w_rest, lb, hgain, ts)
        att = _moba_prompt(q, kbf, vt, kmean)
        mkp, mvp = _memory_kv(mem_prompt, row(norm_mem[l]), wck_b, wcv_b)
        yp = trunk_tail(yp.reshape(B * S, D), att.reshape(B * S, MIX_HALF), hout.reshape(B * S, MIX_HALF),
                        (B, S), mkp, mvp, 1, tm, tm).reshape(B, S, D)
        kp_l.append(k.reshape(B, S, N_HEADS, HEAD_DIM))
        vp_l.append(v.reshape(B, S, N_HEADS, HEAD_DIM))
        sp_l.append(s_fin)
        mkp_l.append(mkp)
        mvp_l.append(mvp)

        q, k, v, hout, s_new = _proj_sample(ys, row(norm_pre_mix[l]), wqk_hi, wqk_lo, w_rest, lb, hgain,
                                            state_hgrn[l], past_len, seqs_per_tile)
        att = _moba_sample(q.reshape(DB, T, MIX_HALF), k, v, cache_k, cache_v, page_table, l, past_len,
                           min(32, page_table.shape[1]))
        ys = trunk_tail(ys.reshape(DB * T, D), att.reshape(DB * T, MIX_HALF), hout, (DB, T),
                        cache_mem_k[l], cache_mem_v[l], min(4, DB), T, tm_s).reshape(DB, T, D)
        ks_l.append(k.reshape(DB, T, N_HEADS, HEAD_DIM))
        vs_l.append(v.reshape(DB, T, N_HEADS, HEAD_DIM))
        ss_l.append(s_new)
    return (yp, ys, jnp.stack(kp_l), jnp.stack(vp_l), jnp.stack(sp_l), jnp.stack(mkp_l), jnp.stack(mvp_l),
            jnp.stack(ks_l), jnp.stack(vs_l), jnp.stack(ss_l))
```

```python
import functools

import jax
import jax.numpy as jnp
from jax import lax
from jax.experimental import pallas as pl
from jax.experimental.pallas import tpu as pltpu

F32 = jnp.float32
BF16 = jnp.bfloat16

NORM_EPS = 1e-6
ROPE_THETA = 10000.0
HEAD_DIM = 128
N_HEADS = 4
MIX_HALF = N_HEADS * HEAD_DIM
MOBA_BLOCK = 256
MOBA_TOPK = 3
PAGE_SIZE = 128
HGRN_CHUNK = 64
HGRN_SUB = 16
XA_HEADS = 4
LOG2_E = 1.4426950408889634
V7X_VMEM_LIMIT = 56 * 1024 * 1024


def _bf(x):
    return x.astype(BF16)


def _mm(a, b):
    return jnp.dot(_bf(a), _bf(b), preferred_element_type=F32)


def _mm_nt(a, b):
    return lax.dot_general(_bf(a), _bf(b), (((1,), (1,)), ((), ())), preferred_element_type=F32)


def _mm_tn(a, b):
    return lax.dot_general(_bf(a), _bf(b), (((0,), (0,)), ((), ())), preferred_element_type=F32)


def _split2(x):
    hi = x.astype(BF16)
    lo = (x - hi.astype(F32)).astype(BF16)
    return hi, lo


def _split3(x):
    a = x.astype(BF16)
    r = x - a.astype(F32)
    b = r.astype(BF16)
    c = (r - b.astype(F32)).astype(BF16)
    return a, b, c


def _mm3(a, b, transpose_b=False):
    a_hi, a_lo = _split2(a)
    b_hi, b_lo = _split2(b)
    dn = (((1,), (1 if transpose_b else 0,)), ((), ()))
    dot = functools.partial(lax.dot_general, dimension_numbers=dn, preferred_element_type=F32)
    return (dot(a_lo, b_hi) + dot(a_hi, b_lo)) + dot(a_hi, b_hi)


def _mm3_nt(a, b):
    return _mm3(a, b, transpose_b=True)


def _rms(x, gain):
    return x * lax.rsqrt(jnp.mean(x * x, axis=-1, keepdims=True) + NORM_EPS) * gain


def _silu(x):
    return x * (1.0 / (1.0 + jnp.exp(-x)))


def _sigmoid(x):
    return 1.0 / (1.0 + jnp.exp(-x))


def _cumsum_rows(g, lmat):
    g1, g2, g3 = _split3(g)
    dot = functools.partial(jnp.dot, preferred_element_type=F32)
    return (dot(lmat, g3) + dot(lmat, g2)) + dot(lmat, g1)


def _chunk_matrices(rows, chunk, sub):
    r = lax.broadcasted_iota(jnp.int32, (rows, rows), 0)
    c = lax.broadcasted_iota(jnp.int32, (rows, rows), 1)
    same_chunk = (r // chunk) == (c // chunk)
    incl = same_chunk & (c <= r)
    start = same_chunk & (c < (r // sub) * sub)
    end = same_chunk & (c < (r // sub + 1) * sub)
    as_bf = lambda m: jnp.where(m, 1.0, 0.0).astype(BF16)
    return as_bf(incl), as_bf(start), as_bf(end)


def _hgrn_chunk(q, k, v, G, Bs, Be, S, sub, value_major):
    c = q.shape[0]
    nsub = c // sub
    qh = q * jnp.exp(G - Bs)
    kd = k * jnp.exp(Bs - G)
    qg = q * jnp.exp(G)
    row = lax.broadcasted_iota(jnp.int32, (c, c), 0)
    col = lax.broadcasted_iota(jnp.int32, (c, c), 1)
    carried = (lambda: _mm_nt(qg, S)) if value_major else (lambda: _mm(qg, S))
    if nsub == 1:
        att = jnp.where(col <= row, _mm_nt(qh, kd), 0.0)
        o = _mm(att, v) + carried()
    else:
        kh = k * jnp.exp(Be - G)
        lhs = [qh]
        z = jnp.zeros((sub, HEAD_DIM), F32)
        rhs_rows = [jnp.concatenate([kd] + [jnp.zeros_like(kd)] * (nsub - 1), axis=1)]
        for j in range(nsub - 1):
            g_end_j = G[(j + 1) * sub - 1:(j + 1) * sub, :]
            lhs.append(q * jnp.exp(jnp.minimum(G - g_end_j, 0.0)))
            parts = [z] * nsub
            parts[j + 1] = kh[j * sub:(j + 1) * sub, :]
            rhs_rows.append(jnp.concatenate(parts, axis=1))
        n_off = (nsub - 1) * sub
        pad = (-(c + n_off)) % HEAD_DIM
        if pad:
            rhs_rows.append(jnp.zeros((pad, nsub * HEAD_DIM), F32))
        R = _mm_nt(jnp.concatenate(lhs, axis=1), jnp.concatenate(rhs_rows, axis=0))
        width = c + n_off + pad
        row2 = lax.broadcasted_iota(jnp.int32, (c, width), 0)
        col2 = lax.broadcasted_iota(jnp.int32, (c, width), 1)
        diag_ok = (col2 < c) & (col2 <= row2) & ((col2 // sub) == (row2 // sub))
        off_ok = (col2 >= c) & (col2 < c + n_off) & (((col2 - c) // sub) < (row2 // sub))
        att2 = jnp.where(diag_ok | off_ok, R, 0.0)
        v_off = v[:n_off, :]
        v_rows = [v, v_off]
        if pad:
            v_rows.append(jnp.zeros((pad, HEAD_DIM), F32))
        if value_major:
            o = _mm(att2, jnp.concatenate(v_rows, axis=0)) + carried()
        else:
            o = _mm(jnp.concatenate([att2, qg], axis=1), jnp.concatenate(v_rows + [S], axis=0))
    g_end = G[c - 1:c, :]
    kk = k * jnp.exp(g_end - G)
    if value_major:
        S_new = jnp.exp(g_end) * S + _mm_tn(v, kk)
    else:
        decay = jnp.broadcast_to(jnp.exp(g_end), (HEAD_DIM, HEAD_DIM)).T
        S_new = decay * S + _mm_tn(kk, v)
    return o, S_new


def _rotary(x, cos, sin):
    return x * cos + pltpu.roll(x, HEAD_DIM // 2, 1) * sin


def _project(x, gain, wqk_hi, wqk_lo, w_rest, cos, sin, lb):
    hn = _rms(x, gain)
    hn_hi, hn_lo = _split2(hn)
    dot = functools.partial(jnp.dot, preferred_element_type=F32)
    qk = (dot(hn_lo, wqk_hi) + dot(hn_hi, wqk_lo)) + dot(hn_hi, wqk_hi)
    rest = dot(hn_hi, w_rest)
    q = jnp.concatenate([_rotary(qk[:, h * HEAD_DIM:(h + 1) * HEAD_DIM], cos, sin) for h in range(N_HEADS)], axis=1)
    k = jnp.concatenate([_rotary(qk[:, MIX_HALF + h * HEAD_DIM:MIX_HALF + (h + 1) * HEAD_DIM], cos, sin)
                         for h in range(N_HEADS)], axis=1)
    v = rest[:, 0:MIX_HALF]
    hq = _silu(rest[:, MIX_HALF:2 * MIX_HALF])
    f = lb + (1.0 - lb) * _sigmoid(rest[:, 2 * MIX_HALF:3 * MIX_HALF])
    hk = 1.0 - f
    g = jnp.log(f)
    hv = rest[:, 3 * MIX_HALF:4 * MIX_HALF]
    hg = rest[:, 4 * MIX_HALF:5 * MIX_HALF]
    return q, k, v, hq, hk, hv, g, hg


def _hgrn_out(o, gain, hg):
    return _rms(o, gain) * _silu(hg)


def _proj_prompt_kernel(x_ref, gain_ref, wqk_hi_ref, wqk_lo_ref, wrest_ref, cos_ref, sin_ref, lb_ref, hgain_ref,
                        q_ref, k_ref, v_ref, kbf_ref, vt_ref, kmean_ref, hout_ref, state_ref, st_ref):
    t = pl.program_id(1)

    @pl.when(t == 0)
    def _():
        st_ref[...] = jnp.zeros_like(st_ref)

    nr, ts, D = x_ref.shape
    cos = jnp.concatenate([cos_ref[...]] * nr, axis=0)
    sin = jnp.concatenate([sin_ref[...]] * nr, axis=0)
    q, k, v, hq, hk, hv, g, hg = _project(x_ref[...].reshape(nr * ts, D), gain_ref[...], wqk_hi_ref[...],
                                          wqk_lo_ref[...], wrest_ref[...], cos, sin, lb_ref[...])
    n_blk = ts // MOBA_BLOCK
    for r in range(nr):
        rows = slice(r * ts, (r + 1) * ts)
        q_ref[r] = q[rows]
        for h in range(N_HEADS):
            k_ref[r, pl.ds(h, ts, stride=N_HEADS), :] = k[rows, h * HEAD_DIM:(h + 1) * HEAD_DIM]
            v_ref[r, pl.ds(h, ts, stride=N_HEADS), :] = v[rows, h * HEAD_DIM:(h + 1) * HEAD_DIM]
        kbf_ref[r] = _bf(k[rows])
        for n in range(n_blk):
            sl = slice(r * ts + n * MOBA_BLOCK, r * ts + (n + 1) * MOBA_BLOCK)
            vt_ref[r, n] = _bf(v[sl, :].T)
            kmean_ref[r, pl.ds(t * n_blk + n, 1), :] = jnp.mean(k[sl, :], axis=0, keepdims=True)

    l_incl, l_start, l_end = _chunk_matrices(ts, HGRN_CHUNK, HGRN_SUB)
    per_row = lambda lmat: jnp.concatenate([_cumsum_rows(g[r * ts:(r + 1) * ts], lmat) for r in range(nr)], axis=0)
    G = per_row(l_incl)
    Bs = per_row(l_start)
    Be = per_row(l_end)
    hgain = hgain_ref[...]
    heads = [slice(h * HEAD_DIM, (h + 1) * HEAD_DIM) for h in range(N_HEADS)]
    states = [[st_ref[r, h] for h in range(N_HEADS)] for r in range(nr)]
    outs = [[[] for _ in range(N_HEADS)] for _ in range(nr)]
    for ci in range(ts // HGRN_CHUNK):
        for r in range(nr):
            rs = slice(r * ts + ci * HGRN_CHUNK, r * ts + (ci + 1) * HGRN_CHUNK)
            for h, hs in enumerate(heads):
                o, states[r][h] = _hgrn_chunk(hq[rs, hs], hk[rs, hs], hv[rs, hs], G[rs, hs], Bs[rs, hs], Be[rs, hs],
                                              states[r][h], HGRN_SUB, True)
                outs[r][h].append(o)
    for r in range(nr):
        for h, hs in enumerate(heads):
            st_ref[r, h] = states[r][h]
            hout_ref[r, :, hs] = _bf(_hgrn_out(jnp.concatenate(outs[r][h], axis=0), hgain, hg[r * ts:(r + 1) * ts, hs]))

    @pl.when(t == pl.num_programs(1) - 1)
    def _():
        for r in range(nr):
            for h in range(N_HEADS):
                state_ref[r, h] = st_ref[r, h].T


def _proj_sample_kernel(x_ref, gain_ref, wqk_hi_ref, wqk_lo_ref, wrest_ref, cos_ref, sin_ref, lb_ref, hgain_ref,
                        s0_ref, q_ref, k_ref, v_ref, hout_ref, state_ref, *, seq_len):
    rows = x_ref.shape[0]
    q, k, v, hq, hk, hv, g, hg = _project(x_ref[...], gain_ref[...], wqk_hi_ref[...], wqk_lo_ref[...], wrest_ref[...],
                                          cos_ref[...], sin_ref[...], lb_ref[...])
    q_ref[...] = q
    for h in range(N_HEADS):
        k_ref[pl.ds(h, rows, stride=N_HEADS), :] = k[:, h * HEAD_DIM:(h + 1) * HEAD_DIM]
        v_ref[pl.ds(h, rows, stride=N_HEADS), :] = v[:, h * HEAD_DIM:(h + 1) * HEAD_DIM]
    l_incl, _, _ = _chunk_matrices(rows, seq_len, seq_len)
    G = _cumsum_rows(g, l_incl)
    zero = jnp.zeros((seq_len, HEAD_DIM), F32)
    hgain = hgain_ref[...]
    for h in range(N_HEADS):
        hs = slice(h * HEAD_DIM, (h + 1) * HEAD_DIM)
        outs = []
        for si in range(rows // seq_len):
            rs = slice(si * seq_len, (si + 1) * seq_len)
            o, S = _hgrn_chunk(hq[rs, hs], hk[rs, hs], hv[rs, hs], G[rs, hs], zero, zero, s0_ref[si, h], seq_len, False)
            state_ref[si, h] = S
            outs.append(o)
        hout_ref[:, hs] = _bf(_hgrn_out(jnp.concatenate(outs, axis=0), hgain, hg[:, hs]))


def _rotary_tables(pos):
    half = HEAD_DIM // 2
    inv_freq = ROPE_THETA ** (-jnp.arange(half, dtype=F32) / half)
    ang = pos.astype(F32)[:, None] * inv_freq[None, :]
    cos = jnp.cos(ang)
    sin = jnp.sin(ang)
    return jnp.concatenate([cos, cos], axis=1), jnp.concatenate([-sin, sin], axis=1)


def _full(shape):
    return pl.BlockSpec(shape, lambda *_: (0,) * len(shape))


def _proj_prompt(x, gain, wqk_hi, wqk_lo, w_rest, lb, hgain, ts):
    B, S, D = x.shape
    nb = S // MOBA_BLOCK
    cos, sin = _rotary_tables(jnp.arange(S))
    out_shape = (
        jax.ShapeDtypeStruct((B, S, MIX_HALF), F32),
        jax.ShapeDtypeStruct((B, S * N_HEADS, HEAD_DIM), F32),
        jax.ShapeDtypeStruct((B, S * N_HEADS, HEAD_DIM), F32),
        jax.ShapeDtypeStruct((B, S, MIX_HALF), BF16),
        jax.ShapeDtypeStruct((B, nb, MIX_HALF, MOBA_BLOCK), BF16),
        jax.ShapeDtypeStruct((B, nb, MIX_HALF), F32),
        jax.ShapeDtypeStruct((B, S, MIX_HALF), BF16),
        jax.ShapeDtypeStruct((B, N_HEADS, HEAD_DIM, HEAD_DIM), F32),
    )
    nr = 2 if B % 2 == 0 else 1
    tok = pl.BlockSpec((nr, ts, MIX_HALF), lambda b, t: (b, t, 0))
    tok_heads = pl.BlockSpec((nr, ts * N_HEADS, HEAD_DIM), lambda b, t: (b, t, 0))
    return pl.pallas_call(
        _proj_prompt_kernel,
        grid=(B // nr, S // ts),
        in_specs=[
            pl.BlockSpec((nr, ts, D), lambda b, t: (b, t, 0)),
            _full((1, D)), _full(wqk_hi.shape), _full(wqk_lo.shape), _full(w_rest.shape),
            pl.BlockSpec((ts, HEAD_DIM), lambda b, t: (t, 0)),
            pl.BlockSpec((ts, HEAD_DIM), lambda b, t: (t, 0)),
            _full((1, MIX_HALF)), _full((1, HEAD_DIM)),
        ],
        out_specs=(
            tok, tok_heads, tok_heads, tok,
            pl.BlockSpec((nr, ts // MOBA_BLOCK, MIX_HALF, MOBA_BLOCK), lambda b, t: (b, t, 0, 0)),
            pl.BlockSpec((nr, nb, MIX_HALF), lambda b, t: (b, 0, 0)),
            tok,
            pl.BlockSpec((nr, N_HEADS, HEAD_DIM, HEAD_DIM), lambda b, t: (b, 0, 0, 0)),
        ),
        out_shape=out_shape,
        scratch_shapes=[pltpu.VMEM((nr, N_HEADS, HEAD_DIM, HEAD_DIM), F32)],
        compiler_params=pltpu.CompilerParams(dimension_semantics=("arbitrary", "arbitrary"),
                                             vmem_limit_bytes=V7X_VMEM_LIMIT),
        name="proj_prompt",
    )(x, gain, wqk_hi, wqk_lo, w_rest, cos, sin, lb, hgain)


def _proj_sample(x, gain, wqk_hi, wqk_lo, w_rest, lb, hgain, state0, past_len, seqs_per_tile):
    DB, T, D = x.shape
    rows = seqs_per_tile * T
    cos, sin = _rotary_tables(past_len + jnp.arange(T))
    cos = jnp.tile(cos, (seqs_per_tile, 1))
    sin = jnp.tile(sin, (seqs_per_tile, 1))
    out_shape = (
        jax.ShapeDtypeStruct((DB * T, MIX_HALF), F32),
        jax.ShapeDtypeStruct((DB * T * N_HEADS, HEAD_DIM), F32),
        jax.ShapeDtypeStruct((DB * T * N_HEADS, HEAD_DIM), F32),
        jax.ShapeDtypeStruct((DB * T, MIX_HALF), BF16),
        jax.ShapeDtypeStruct((DB, N_HEADS, HEAD_DIM, HEAD_DIM), F32),
    )
    tok = pl.BlockSpec((rows, MIX_HALF), lambda i: (i, 0))
    tok_heads = pl.BlockSpec((rows * N_HEADS, HEAD_DIM), lambda i: (i, 0))
    st = pl.BlockSpec((seqs_per_tile, N_HEADS, HEAD_DIM, HEAD_DIM), lambda i: (i, 0, 0, 0))
    return pl.pallas_call(
        functools.partial(_proj_sample_kernel, seq_len=T),
        grid=(DB // seqs_per_tile,),
        in_specs=[
            pl.BlockSpec((rows, D), lambda i: (i, 0)),
            _full((1, D)), _full(wqk_hi.shape), _full(wqk_lo.shape), _full(w_rest.shape),
            _full((rows, HEAD_DIM)), _full((rows, HEAD_DIM)),
            _full((1, MIX_HALF)), _full((1, HEAD_DIM)),
            st,
        ],
        out_specs=(tok, tok_heads, tok_heads, tok, st),
        out_shape=out_shape,
        compiler_params=pltpu.CompilerParams(dimension_semantics=("arbitrary",),
                                             vmem_limit_bytes=V7X_VMEM_LIMIT),
        name="proj_sample",
    )(x.reshape(DB * T, D), gain, wqk_hi, wqk_lo, w_rest, cos, sin, lb, hgain, state0)


def _top_blocks(gate, n_valid, k_top, axis):
    nb = gate.shape[axis]
    blk = lax.broadcasted_iota(jnp.int32, gate.shape, axis)
    neg = jnp.float32(-jnp.inf)
    g = jnp.where(blk < n_valid, gate, neg)
    sel = jnp.zeros(gate.shape, F32)
    for _ in range(k_top):
        m = jnp.max(g, axis=axis, keepdims=True)
        first = jnp.min(jnp.where(g == m, blk, nb), axis=axis, keepdims=True)
        pick = blk == first
        sel = jnp.where(pick & (m > neg), 1.0, sel)
        g = jnp.where(pick, neg, g)
    return sel


def _moba_prompt_kernel(q_ref, kbf_ref, vt_ref, kmean_ref, o_ref, sel_ref, sa_ref, sb_ref, *, k_top, group, heads):
    i = pl.program_id(2)
    neg = jnp.float32(-jnp.inf)
    start = pl.multiple_of(i * MOBA_BLOCK, MOBA_BLOCK)
    lanes = [slice(hh * HEAD_DIM, (hh + 1) * HEAD_DIM) for hh in range(heads)]
    rows = group * MOBA_BLOCK
    last_group = kbf_ref.shape[1] // rows - 1

    qts, ms, ls, accs = [], [], [], []
    for hh, hl in enumerate(lanes):
        q_t = q_ref[0, :, hl].T
        chosen = _top_blocks(_mm3(kmean_ref[0, :, hl], q_t), i, k_top, 0)
        sel_ref[hh] = jnp.where(chosen > 0.0, 0.0, neg)
        qt = _bf(q_t * (HEAD_DIM ** -0.5 * LOG2_E))
        s = jnp.dot(kbf_ref[0, pl.ds(start, MOBA_BLOCK), hl], qt, preferred_element_type=F32)
        krow = lax.broadcasted_iota(jnp.int32, s.shape, 0)
        qcol = lax.broadcasted_iota(jnp.int32, s.shape, 1)
        s = jnp.where(krow <= qcol, s, neg)
        m = jnp.max(s, axis=0, keepdims=True)
        p = jnp.exp2(s - m)
        qts.append(qt)
        ms.append(m)
        ls.append(jnp.sum(p, axis=0, keepdims=True))
        accs.append(jnp.dot(vt_ref[0, i, hl, :], _bf(p), preferred_element_type=F32))

    def score(g, buf, hh):
        g = jnp.minimum(g, last_group)
        st = pl.multiple_of(g * rows, rows)
        s = jnp.dot(kbf_ref[0, pl.ds(st, rows), lanes[hh]], qts[hh], preferred_element_type=F32)
        mg = None
        for u in range(group):
            su = s[u * MOBA_BLOCK:(u + 1) * MOBA_BLOCK, :] + sel_ref[hh, pl.ds(g * group + u, 1), :]
            buf[hh, u * MOBA_BLOCK:(u + 1) * MOBA_BLOCK, :] = su
            mu = jnp.max(su, axis=0, keepdims=True)
            mg = mu if mg is None else jnp.maximum(mg, mu)
        return mg

    def absorb(g, buf, hh, mg, m, l, acc):
        m_new = jnp.maximum(m, mg)
        alpha = jnp.exp2(m - m_new)
        l = alpha * l
        acc = alpha * acc
        for u in range(group):
            p = jnp.exp2(buf[hh, u * MOBA_BLOCK:(u + 1) * MOBA_BLOCK, :] - m_new)
            l = l + jnp.sum(p, axis=0, keepdims=True)
            acc = acc + jnp.dot(vt_ref[0, g * group + u, lanes[hh], :], _bf(p), preferred_element_type=F32)
        return m_new, l, acc

    def body(k, carry):
        state, m_a = carry
        m_b = [score(2 * k + 1, sb_ref, hh) for hh in range(heads)]
        state = [absorb(2 * k, sa_ref, hh, m_a[hh], *state[hh]) for hh in range(heads)]
        m_a = [score(2 * k + 2, sa_ref, hh) for hh in range(heads)]
        state = [absorb(2 * k + 1, sb_ref, hh, m_b[hh], *state[hh]) for hh in range(heads)]
        return state, m_a

    n_groups = (i + group - 1) // group
    first = [score(0, sa_ref, hh) for hh in range(heads)]
    state, _ = lax.fori_loop(0, (n_groups + 1) // 2, body, ([(ms[hh], ls[hh], accs[hh]) for hh in range(heads)], first))
    for hh, hl in enumerate(lanes):
        _, l, acc = state[hh]
        o_ref[0, :, hl] = _bf((acc / l).T)


def _moba_prompt(q, kbf, vt, kmean):
    B, S, _ = q.shape
    nb = S // MOBA_BLOCK
    k_top = min(MOBA_TOPK, nb - 1)
    group = 2 if nb % 4 == 0 else 1
    assert nb % (2 * group) == 0, "past blocks are absorbed two groups per loop trip"
    heads = 2
    width = heads * HEAD_DIM
    return pl.pallas_call(
        functools.partial(_moba_prompt_kernel, k_top=k_top, group=group, heads=heads),
        grid=(B, N_HEADS // heads, nb),
        in_specs=[
            pl.BlockSpec((1, MOBA_BLOCK, width), lambda b, h, i: (b, i, h)),
            pl.BlockSpec((1, S, width), lambda b, h, i: (b, 0, h)),
            pl.BlockSpec((1, nb, width, MOBA_BLOCK), lambda b, h, i: (b, 0, h, 0)),
            pl.BlockSpec((1, nb, width), lambda b, h, i: (b, 0, h)),
        ],
        out_specs=pl.BlockSpec((1, MOBA_BLOCK, width), lambda b, h, i: (b, i, h)),
        out_shape=jax.ShapeDtypeStruct((B, S, MIX_HALF), BF16),
        scratch_shapes=[pltpu.VMEM((heads, nb, MOBA_BLOCK), F32),
                        pltpu.VMEM((heads, group * MOBA_BLOCK, MOBA_BLOCK), F32),
                        pltpu.VMEM((heads, group * MOBA_BLOCK, MOBA_BLOCK), F32)],
        compiler_params=pltpu.CompilerParams(dimension_semantics=("arbitrary", "arbitrary", "arbitrary"),
                                             vmem_limit_bytes=V7X_VMEM_LIMIT),
        name="moba_prompt",
    )(q, kbf, vt, kmean)


def _heads_to_lanes(ref):
    tokens = ref.shape[0] // N_HEADS
    return jnp.concatenate([ref[pl.ds(h, tokens, stride=N_HEADS), :] for h in range(N_HEADS)], axis=1)


def _moba_sample_scores_kernel(pt_ref, q_ref, kn_ref, vn_ref, expand_ref, *refs, pages_per_step, n_steps, k_top):
    kp_refs = refs[:pages_per_step]
    p_ref, acc0_ref, linv_ref = refs[pages_per_step:pages_per_step + 3]
    qbd_ref, s_ref, ksum_ref = refs[pages_per_step + 3:]
    del pt_ref
    step = pl.program_id(1)
    T = q_ref.shape[1]
    npair = N_HEADS * T
    n_pages = s_ref.shape[0]
    pages_per_blk = MOBA_BLOCK // PAGE_SIZE
    blks_per_step = pages_per_step // pages_per_blk
    neg = jnp.float32(-jnp.inf)

    @pl.when(step == 0)
    def _():
        pair_head = lax.broadcasted_iota(jnp.int32, (npair, MIX_HALF), 0) // T
        lane_head = lax.broadcasted_iota(jnp.int32, (npair, MIX_HALF), 1) // HEAD_DIM
        q4 = jnp.concatenate([q_ref[0]] * N_HEADS, axis=0)
        qbd_ref[...] = jnp.where(pair_head == lane_head, q4, 0.0)

    qs = _bf(qbd_ref[...] * (HEAD_DIM ** -0.5))
    for n in range(blks_per_step):
        ksum = jnp.zeros((1, MIX_HALF), F32)
        for pp in range(pages_per_blk):
            p = n * pages_per_blk + pp
            kp = _heads_to_lanes(kp_refs[p])
            s_ref[step * pages_per_step + p] = _mm_nt(qs, kp)
            ksum = ksum + jnp.sum(kp, axis=0, keepdims=True)
        ksum_ref[pl.ds(step * blks_per_step + n, 1), :] = ksum

    @pl.when(step == n_steps - 1)
    def _():
        n_blocks = ksum_ref.shape[0]
        kmean = ksum_ref[...] * (1.0 / MOBA_BLOCK)
        sel = _top_blocks(_mm3_nt(kmean, qbd_ref[...]), n_blocks, k_top, 0)
        chosen = _mm_tn(sel, expand_ref[...])
        s_own = _mm_nt(qs, _heads_to_lanes(kn_ref))
        key_t = lax.broadcasted_iota(jnp.int32, (npair, T), 1)
        qry_t = lax.broadcasted_iota(jnp.int32, (npair, T), 0) % T
        s_own = jnp.where(key_t <= qry_t, s_own, neg)

        def masked(p):
            return jnp.where(chosen[:, p * PAGE_SIZE:(p + 1) * PAGE_SIZE] > 0.5, s_ref[p], neg)

        m_lanes = masked(0)
        for p in range(1, n_pages):
            m_lanes = jnp.maximum(m_lanes, masked(p))
        m = jnp.maximum(jnp.max(m_lanes, axis=1, keepdims=True), jnp.max(s_own, axis=1, keepdims=True))
        l_lanes = jnp.zeros((npair, PAGE_SIZE), F32)
        for p in range(n_pages):
            w = jnp.exp(masked(p) - m)
            l_lanes = l_lanes + w
            p_ref[0, p] = _bf(w)
        w_own = jnp.exp(s_own - m)
        l = jnp.sum(l_lanes, axis=1, keepdims=True) + jnp.sum(w_own, axis=1, keepdims=True)
        acc0_ref[0] = _mm(w_own, _heads_to_lanes(vn_ref))
        linv_ref[0] = jnp.broadcast_to(1.0 / l, (npair, HEAD_DIM))


def _moba_sample_values_kernel(pt_ref, p_ref, acc0_ref, linv_ref, *refs, pages_per_step, n_steps):
    vp_refs = refs[:pages_per_step]
    o_ref = refs[pages_per_step]
    acc_ref = refs[pages_per_step + 1]
    del pt_ref
    step = pl.program_id(1)
    T = o_ref.shape[1]

    @pl.when(step == 0)
    def _():
        acc_ref[...] = acc0_ref[0]

    acc = acc_ref[...]
    for p in range(pages_per_step):
        vp = _bf(_heads_to_lanes(vp_refs[p]))
        acc = acc + jnp.dot(p_ref[0, step * pages_per_step + p], vp, preferred_element_type=F32)
    acc_ref[...] = acc

    @pl.when(step == n_steps - 1)
    def _():
        linv = linv_ref[0]
        o_ref[0] = jnp.concatenate(
            [acc[h * T:(h + 1) * T, h * HEAD_DIM:(h + 1) * HEAD_DIM] * linv[h * T:(h + 1) * T, :] for h in range(N_HEADS)],
            axis=1)


def _moba_sample(q, k_new, v_new, cache_k, cache_v, page_table, layer, past_len, pages_per_step):
    DB, T, _ = q.shape
    n_pages = page_table.shape[1]
    pages_per_blk = MOBA_BLOCK // PAGE_SIZE
    n_full = past_len // MOBA_BLOCK
    assert n_pages == n_full * pages_per_blk, "the current block must hold no cached rows"
    assert n_pages % pages_per_step == 0 and pages_per_step % pages_per_blk == 0
    n_steps = n_pages // pages_per_step
    k_top = min(MOBA_TOPK, n_full)
    npair = N_HEADS * T
    n_keys = n_pages * PAGE_SIZE
    pt = page_table.astype(jnp.int32).reshape(DB * n_pages)
    expand = (jnp.arange(n_keys)[None, :] // MOBA_BLOCK == jnp.arange(n_full)[:, None]).astype(BF16)

    def page_map(p):
        return lambda b, s, pt_ref: (layer * n_pool + pt_ref[b * n_pages + s * pages_per_step + p], 0)

    per_seq = lambda *tail: (lambda b, s, pt_ref: (b,) + tail)
    page = (PAGE_SIZE * N_HEADS, HEAD_DIM)
    n_pool = cache_k.shape[1]
    cache_k = cache_k.reshape(-1, HEAD_DIM)
    cache_v = cache_v.reshape(-1, HEAD_DIM)
    new_tok = pl.BlockSpec((T * N_HEADS, HEAD_DIM), per_seq(0))
    p_spec = pl.BlockSpec((1, n_pages, npair, PAGE_SIZE), per_seq(0, 0, 0))
    acc_spec = pl.BlockSpec((1, npair, MIX_HALF), per_seq(0, 0))
    linv_spec = pl.BlockSpec((1, npair, HEAD_DIM), per_seq(0, 0))
    params = pltpu.CompilerParams(dimension_semantics=("arbitrary", "arbitrary"), vmem_limit_bytes=V7X_VMEM_LIMIT)

    weights, acc0, linv = pl.pallas_call(
        functools.partial(_moba_sample_scores_kernel, pages_per_step=pages_per_step, n_steps=n_steps, k_top=k_top),
        grid_spec=pltpu.PrefetchScalarGridSpec(
            num_scalar_prefetch=1,
            grid=(DB, n_steps),
            in_specs=[pl.BlockSpec((1, T, MIX_HALF), per_seq(0, 0)), new_tok, new_tok,
                      pl.BlockSpec((n_full, n_keys), lambda b, s, pt_ref: (0, 0))]
            + [pl.BlockSpec(page, page_map(p)) for p in range(pages_per_step)],
            out_specs=(p_spec, acc_spec, linv_spec),
            scratch_shapes=[
                pltpu.VMEM((npair, MIX_HALF), F32),
                pltpu.VMEM((n_pages, npair, PAGE_SIZE), F32),
                pltpu.VMEM((n_full, MIX_HALF), F32),
            ],
        ),
        out_shape=(jax.ShapeDtypeStruct((DB, n_pages, npair, PAGE_SIZE), BF16),
                   jax.ShapeDtypeStruct((DB, npair, MIX_HALF), F32),
                   jax.ShapeDtypeStruct((DB, npair, HEAD_DIM), F32)),
        compiler_params=params,
        name="moba_sample_scores",
    )(pt, q, k_new, v_new, expand, *([cache_k] * pages_per_step))

    return pl.pallas_call(
        functools.partial(_moba_sample_values_kernel, pages_per_step=pages_per_step, n_steps=n_steps),
        grid_spec=pltpu.PrefetchScalarGridSpec(
            num_scalar_prefetch=1,
            grid=(DB, n_steps),
            in_specs=[p_spec, acc_spec, linv_spec] + [pl.BlockSpec(page, page_map(p)) for p in range(pages_per_step)],
            out_specs=pl.BlockSpec((1, T, MIX_HALF), per_seq(0, 0)),
            scratch_shapes=[pltpu.VMEM((npair, MIX_HALF), F32)],
        ),
        out_shape=jax.ShapeDtypeStruct((DB, T, MIX_HALF), F32),
        compiler_params=params,
        name="moba_sample_values",
    )(pt, weights, acc0, linv, *([cache_v] * pages_per_step))


def _memory_kv_kernel(mem_ref, gain_ref, wk_ref, wv_ref, k_ref, v_ref):
    mn = _bf(_rms(mem_ref[0], gain_ref[...]))
    k = jnp.dot(mn, wk_ref[...], preferred_element_type=F32)
    v = jnp.dot(mn, wv_ref[...], preferred_element_type=F32)
    dh = k_ref.shape[-1]
    for h in range(XA_HEADS):
        k_ref[0, :, h, :] = k[:, h * dh:(h + 1) * dh]
        v_ref[0, :, h, :] = v[:, h * dh:(h + 1) * dh]


def _memory_kv(mem, gain, w_ck, w_cv):
    B, M, D = mem.shape
    dh = w_ck.shape[1] // XA_HEADS
    blk = pl.BlockSpec((1, M, D), lambda b: (b, 0, 0))
    out = jax.ShapeDtypeStruct((B, M, XA_HEADS, dh), F32)
    oblk = pl.BlockSpec((1, M, XA_HEADS, dh), lambda b: (b, 0, 0, 0))
    return pl.pallas_call(
        _memory_kv_kernel,
        grid=(B,),
        in_specs=[blk, _full((1, D)), _full(w_ck.shape), _full(w_cv.shape)],
        out_specs=(oblk, oblk),
        out_shape=(out, out),
        compiler_params=pltpu.CompilerParams(dimension_semantics=("arbitrary",), vmem_limit_bytes=V7X_VMEM_LIMIT),
        name="memory_kv",
    )(mem, gain, w_ck, w_cv)


def _mix_kernel(x_ref, att_ref, hout_ref, wo_ref, gpost_ref, gpre_ref, wq_ref, h_ref, q_ref):
    half = att_ref.shape[-1]
    mix = (jnp.dot(_bf(att_ref[...]), wo_ref[0:half, :], preferred_element_type=F32)
           + jnp.dot(hout_ref[...], wo_ref[half:, :], preferred_element_type=F32))
    h = x_ref[...] + _rms(mix, gpost_ref[...])
    h_ref[...] = h
    q = jnp.dot(_bf(_rms(h, gpre_ref[...])), wq_ref[...], preferred_element_type=F32)
    q_ref[...] = _bf(q * (q.shape[-1] // XA_HEADS) ** -0.5)


def _mix(x, att, hout, w_out, g_post, g_pre_x, w_cq, tm):
    N, D = x.shape
    half = att.shape[1]
    return pl.pallas_call(
        _mix_kernel,
        grid=(N // tm,),
        in_specs=[
            pl.BlockSpec((tm, D), lambda i: (i, 0)),
            pl.BlockSpec((tm, half), lambda i: (i, 0)),
            pl.BlockSpec((tm, half), lambda i: (i, 0)),
            _full(w_out.shape), _full((1, D)), _full((1, D)), _full(w_cq.shape),
        ],
        out_specs=(pl.BlockSpec((tm, D), lambda i: (i, 0)), pl.BlockSpec((tm, w_cq.shape[1]), lambda i: (i, 0))),
        out_shape=(jax.ShapeDtypeStruct((N, D), F32), jax.ShapeDtypeStruct((N, w_cq.shape[1]), BF16)),
        compiler_params=pltpu.CompilerParams(dimension_semantics=("arbitrary",), vmem_limit_bytes=V7X_VMEM_LIMIT),
        name="mix_out",
    )(x, att, hout, w_out, g_post, g_pre_x, w_cq)


def _xattn_kernel(q_ref, mk_ref, mv_ref, o_ref):
    groups = q_ref.shape[0]
    dh = q_ref.shape[-1] // XA_HEADS
    pieces = dh // HEAD_DIM
    stride = pieces * XA_HEADS
    M = mk_ref.shape[1] // stride

    def head(ref, gi, h):
        return jnp.concatenate([ref[gi, pl.ds(c * XA_HEADS + h, M, stride=stride), :] for c in range(pieces)], axis=1)

    for gi in range(groups):
        outs = []
        for h in range(XA_HEADS):
            s = _mm_nt(q_ref[gi, :, h * dh:(h + 1) * dh], head(mk_ref, gi, h))
            p = jnp.exp(s - jnp.max(s, axis=-1, keepdims=True))
            l = jnp.sum(p, axis=-1, keepdims=True)
            outs.append(_mm(p, head(mv_ref, gi, h)) / l)
        o_ref[gi] = _bf(jnp.concatenate(outs, axis=1))


def _lane_rows(mem):
    NG, M, H, dh = mem.shape
    pieces = dh // HEAD_DIM
    return mem.reshape(NG, M, H, pieces, HEAD_DIM).transpose(0, 1, 3, 2, 4).reshape(NG, M * pieces * H, HEAD_DIM)


def _xattn(q, mk, mv, groups_per_tile, rows_per_tile):
    NG, R, W = q.shape
    mk = _lane_rows(mk)
    mv = _lane_rows(mv)
    mem_blk = (groups_per_tile,) + mk.shape[1:]
    return pl.pallas_call(
        _xattn_kernel,
        grid=(NG // groups_per_tile, R // rows_per_tile),
        in_specs=[
            pl.BlockSpec((groups_per_tile, rows_per_tile, W), lambda g, r: (g, r, 0)),
            pl.BlockSpec(mem_blk, lambda g, r: (g, 0, 0)),
            pl.BlockSpec(mem_blk, lambda g, r: (g, 0, 0)),
        ],
        out_specs=pl.BlockSpec((groups_per_tile, rows_per_tile, W), lambda g, r: (g, r, 0)),
        out_shape=jax.ShapeDtypeStruct((NG, R, W), BF16),
        compiler_params=pltpu.CompilerParams(dimension_semantics=("arbitrary", "arbitrary"),
                                             vmem_limit_bytes=V7X_VMEM_LIMIT),
        name="cross_attn",
    )(q, mk, mv)


def _ffn_kernel(h_ref, o_ref, wco_ref, gpx_ref, gpf_ref, wup_ref, wdn_ref, gpo_ref, y_ref, *, ff_chunk):
    h = h_ref[...] + _rms(jnp.dot(o_ref[...], wco_ref[...], preferred_element_type=F32), gpx_ref[...])
    hn = _bf(_rms(h, gpf_ref[...]))
    acc = jnp.zeros(h.shape, F32)
    for c in range(wup_ref.shape[1] // ff_chunk):
        cs = slice(c * ff_chunk, (c + 1) * ff_chunk)
        up = jnp.maximum(jnp.dot(hn, wup_ref[:, cs], preferred_element_type=F32), 0.0)
        acc = acc + jnp.dot(_bf(up * up), wdn_ref[cs, :], preferred_element_type=F32)
    y_ref[...] = h + _rms(acc, gpo_ref[...])


def _ffn(h, o, w_co, g_post_x, g_pre_ffn, w_up, w_down, g_post_ffn, tm, ff_chunk):
    N, D = h.shape
    tok = pl.BlockSpec((tm, D), lambda i: (i, 0))
    return pl.pallas_call(
        functools.partial(_ffn_kernel, ff_chunk=ff_chunk),
        grid=(N // tm,),
        in_specs=[tok, pl.BlockSpec((tm, o.shape[1]), lambda i: (i, 0)),
                  _full(w_co.shape), _full((1, D)), _full((1, D)), _full(w_up.shape), _full(w_down.shape),
                  _full((1, D))],
        out_specs=tok,
        out_shape=jax.ShapeDtypeStruct((N, D), F32),
        compiler_params=pltpu.CompilerParams(dimension_semantics=("arbitrary",), vmem_limit_bytes=V7X_VMEM_LIMIT),
        name="co_ffn",
    )(h, o, w_co, g_post_x, g_pre_ffn, w_up, w_down, g_post_ffn)


def _tiles(n_prompt_seq):
    ts = MOBA_BLOCK
    tm = min(512, n_prompt_seq)
    return ts, tm


def kernel(x_prompt, x_sample, cache_k, cache_v, state_hgrn, cache_mem_k, cache_mem_v, page_table, mem_prompt, norm_pre_mix, w_in, hgrn_lb, hgrn_norm, w_out, norm_post_mix, norm_mem, norm_pre_x, w_cq, w_ck, w_cv, w_co, norm_post_x, norm_pre_ffn, w_up, w_down, norm_post_ffn):
    B, S, D = x_prompt.shape
    DB, T, _ = x_sample.shape
    depth = w_in.shape[0]
    past_len = page_table.shape[1] * PAGE_SIZE
    assert D == 2 * MIX_HALF and S % MOBA_BLOCK == 0 and cache_k.shape[2] == PAGE_SIZE
    ts, tm = _tiles(S)
    tm_s = min(tm, DB * T)
    seqs_per_tile = min(8, DB)
    xa_w = w_cq.shape[2]

    lb_all = jnp.cumsum(jax.nn.softmax(hgrn_lb.astype(F32), axis=0), axis=0)
    row = lambda a: a.reshape(1, -1)

    yp, ys = x_prompt, x_sample
    kp_l, vp_l, sp_l, mkp_l, mvp_l, ks_l, vs_l, ss_l = [], [], [], [], [], [], [], []
    for l in range(depth):
        w_qk = w_in[l][:, :2 * MIX_HALF]
        wqk_hi = _bf(w_qk)
        wqk_lo = _bf(w_qk - wqk_hi.astype(F32))
        w_rest = _bf(w_in[l][:, 2 * MIX_HALF:])
        lb = row(lb_all[l])
        hgain = row(hgrn_norm[l])
        wo_b, wcq_b, wck_b, wcv_b, wco_b = _bf(w_out[l]), _bf(w_cq[l]), _bf(w_ck[l]), _bf(w_cv[l]), _bf(w_co[l])
        wup_b, wdn_b = _bf(w_up[l]), _bf(w_down[l])

        def trunk_tail(x2d, att2d, hout2d, q_groups, mk, mv, groups_per_tile, rows_per_tile, tile):
            h1, qx = _mix(x2d, att2d, hout2d, wo_b, row(norm_post_mix[l]), row(norm_pre_x[l]), wcq_b, tile)
            ox = _xattn(qx.reshape(q_groups + (xa_w,)), mk, mv, groups_per_tile, rows_per_tile)
            return _ffn(h1, ox.reshape(-1, xa_w), wco_b, row(norm_post_x[l]), row(norm_pre_ffn[l]), wup_b, wdn_b,
                        row(norm_post_ffn[l]), tile, 1024)

        q, k, v, kbf, vt, kmean, hout, s_fin = _proj_prompt(yp, row(norm_pre_mix[l]), wqk_hi, wqk_lo, w_rest, lb, hgain, ts)
        att = _moba_prompt(q, kbf, vt, kmean)
        mkp, mvp = _memory_kv(mem_prompt, row(norm_mem[l]), wck_b, wcv_b)
        yp = trunk_tail(yp.reshape(B * S, D), att.reshape(B * S, MIX_HALF), hout.reshape(B * S, MIX_HALF),
                        (B, S), mkp, mvp, 1, tm, tm).reshape(B, S, D)
        kp_l.append(k.reshape(B, S, N_HEADS, HEAD_DIM))
        vp_l.append(v.reshape(B, S, N_HEADS, HEAD_DIM))
        sp_l.append(s_fin)
        mkp_l.append(mkp)
        mvp_l.append(mvp)

        q, k, v, hout, s_new = _proj_sample(ys, row(norm_pre_mix[l]), wqk_hi, wqk_lo, w_rest, lb, hgain,
                                            state_hgrn[l], past_len, seqs_per_tile)
        att = _moba_sample(q.reshape(DB, T, MIX_HALF), k, v, cache_k, cache_v, page_table, l, past_len,
                           min(64, page_table.shape[1]))
        ys = trunk_tail(ys.reshape(DB * T, D), att.reshape(DB * T, MIX_HALF), hout, (DB, T),
                        cache_mem_k[l], cache_mem_v[l], min(4, DB), T, tm_s).reshape(DB, T, D)
        ks_l.append(k.reshape(DB, T, N_HEADS, HEAD_DIM))
        vs_l.append(v.reshape(DB, T, N_HEADS, HEAD_DIM))
        ss_l.append(s_new)
    return (yp, ys, jnp.stack(kp_l), jnp.stack(vp_l), jnp.stack(sp_l), jnp.stack(mkp_l), jnp.stack(mvp_l),
            jnp.stack(ks_l), jnp.stack(vs_l), jnp.stack(ss_l))
```

```python
import functools

import jax
import jax.numpy as jnp
from jax import lax
from jax.experimental import pallas as pl
from jax.experimental.pallas import tpu as pltpu

F32 = jnp.float32
BF16 = jnp.bfloat16

NORM_EPS = 1e-6
ROPE_THETA = 10000.0
HEAD_DIM = 128
N_HEADS = 4
MIX_HALF = N_HEADS * HEAD_DIM
MOBA_BLOCK = 256
MOBA_TOPK = 3
PAGE_SIZE = 128
HGRN_CHUNK = 64
HGRN_SUB = 16
XA_HEADS = 4
LOG2_E = 1.4426950408889634
V7X_VMEM_LIMIT = 56 * 1024 * 1024


def _bf(x):
    return x.astype(BF16)


def _mm(a, b):
    return jnp.dot(_bf(a), _bf(b), preferred_element_type=F32)


def _mm_nt(a, b):
    return lax.dot_general(_bf(a), _bf(b), (((1,), (1,)), ((), ())), preferred_element_type=F32)


def _mm_tn(a, b):
    return lax.dot_general(_bf(a), _bf(b), (((0,), (0,)), ((), ())), preferred_element_type=F32)


def _split2(x):
    hi = x.astype(BF16)
    lo = (x - hi.astype(F32)).astype(BF16)
    return hi, lo


def _split3(x):
    a = x.astype(BF16)
    r = x - a.astype(F32)
    b = r.astype(BF16)
    c = (r - b.astype(F32)).astype(BF16)
    return a, b, c


def _mm3(a, b, transpose_b=False):
    a_hi, a_lo = _split2(a)
    b_hi, b_lo = _split2(b)
    dn = (((1,), (1 if transpose_b else 0,)), ((), ()))
    dot = functools.partial(lax.dot_general, dimension_numbers=dn, preferred_element_type=F32)
    return (dot(a_lo, b_hi) + dot(a_hi, b_lo)) + dot(a_hi, b_hi)


def _mm3_nt(a, b):
    return _mm3(a, b, transpose_b=True)


def _rms(x, gain):
    return x * lax.rsqrt(jnp.mean(x * x, axis=-1, keepdims=True) + NORM_EPS) * gain


def _silu(x):
    return x * (1.0 / (1.0 + jnp.exp(-x)))


def _sigmoid(x):
    return 1.0 / (1.0 + jnp.exp(-x))


def _cumsum_rows(g, lmat):
    g1, g2, g3 = _split3(g)
    dot = functools.partial(jnp.dot, preferred_element_type=F32)
    return (dot(lmat, g3) + dot(lmat, g2)) + dot(lmat, g1)


def _chunk_matrices(rows, chunk, sub):
    r = lax.broadcasted_iota(jnp.int32, (rows, rows), 0)
    c = lax.broadcasted_iota(jnp.int32, (rows, rows), 1)
    same_chunk = (r // chunk) == (c // chunk)
    incl = same_chunk & (c <= r)
    start = same_chunk & (c < (r // sub) * sub)
    end = same_chunk & (c < (r // sub + 1) * sub)
    as_bf = lambda m: jnp.where(m, 1.0, 0.0).astype(BF16)
    return as_bf(incl), as_bf(start), as_bf(end)


def _hgrn_chunk(q, k, v, G, Bs, Be, S, sub, value_major):
    c = q.shape[0]
    nsub = c // sub
    qh = q * jnp.exp(G - Bs)
    kd = k * jnp.exp(Bs - G)
    qg = q * jnp.exp(G)
    row = lax.broadcasted_iota(jnp.int32, (c, c), 0)
    col = lax.broadcasted_iota(jnp.int32, (c, c), 1)
    carried = (lambda: _mm_nt(qg, S)) if value_major else (lambda: _mm(qg, S))
    if nsub == 1:
        att = jnp.where(col <= row, _mm_nt(qh, kd), 0.0)
        o = _mm(att, v) + carried()
    else:
        kh = k * jnp.exp(Be - G)
        lhs = [qh]
        z = jnp.zeros((sub, HEAD_DIM), F32)
        rhs_rows = [jnp.concatenate([kd] + [jnp.zeros_like(kd)] * (nsub - 1), axis=1)]
        for j in range(nsub - 1):
            g_end_j = G[(j + 1) * sub - 1:(j + 1) * sub, :]
            lhs.append(q * jnp.exp(jnp.minimum(G - g_end_j, 0.0)))
            parts = [z] * nsub
            parts[j + 1] = kh[j * sub:(j + 1) * sub, :]
            rhs_rows.append(jnp.concatenate(parts, axis=1))
        n_off = (nsub - 1) * sub
        R = _mm_nt(jnp.concatenate(lhs, axis=1), jnp.concatenate(rhs_rows, axis=0))
        width = c + n_off
        row2 = lax.broadcasted_iota(jnp.int32, (c, width), 0)
        col2 = lax.broadcasted_iota(jnp.int32, (c, width), 1)
        diag_ok = (col2 < c) & (col2 <= row2) & ((col2 // sub) == (row2 // sub))
        off_ok = (col2 >= c) & (col2 < c + n_off) & (((col2 - c) // sub) < (row2 // sub))
        att2 = jnp.where(diag_ok | off_ok, R, 0.0)
        v_rows = [v, v[:n_off, :]]
        if value_major:
            o = _mm(att2, jnp.concatenate(v_rows, axis=0)) + carried()
        else:
            o = _mm(jnp.concatenate([att2, qg], axis=1), jnp.concatenate(v_rows + [S], axis=0))
    g_end = G[c - 1:c, :]
    kk = k * jnp.exp(g_end - G)
    if value_major:
        S_new = jnp.exp(g_end) * S + _mm_tn(v, kk)
    else:
        decay = jnp.broadcast_to(jnp.exp(g_end), (HEAD_DIM, HEAD_DIM)).T
        S_new = decay * S + _mm_tn(kk, v)
    return o, S_new


def _rotary(x, cos, sin):
    return x * cos + pltpu.roll(x, HEAD_DIM // 2, 1) * sin


def _project(x, gain, wqk_hi, wqk_lo, w_rest, cos, sin, lb):
    hn = _rms(x, gain)
    hn_hi, hn_lo = _split2(hn)
    dot = functools.partial(jnp.dot, preferred_element_type=F32)
    qk = (dot(hn_lo, wqk_hi) + dot(hn_hi, wqk_lo)) + dot(hn_hi, wqk_hi)
    rest = dot(hn_hi, w_rest)
    q = jnp.concatenate([_rotary(qk[:, h * HEAD_DIM:(h + 1) * HEAD_DIM], cos, sin) for h in range(N_HEADS)], axis=1)
    k = jnp.concatenate([_rotary(qk[:, MIX_HALF + h * HEAD_DIM:MIX_HALF + (h + 1) * HEAD_DIM], cos, sin)
                         for h in range(N_HEADS)], axis=1)
    v = rest[:, 0:MIX_HALF]
    hq = _silu(rest[:, MIX_HALF:2 * MIX_HALF])
    f = lb + (1.0 - lb) * _sigmoid(rest[:, 2 * MIX_HALF:3 * MIX_HALF])
    hk = 1.0 - f
    g = jnp.log(f)
    hv = rest[:, 3 * MIX_HALF:4 * MIX_HALF]
    hg = rest[:, 4 * MIX_HALF:5 * MIX_HALF]
    return q, k, v, hq, hk, hv, g, hg


def _hgrn_out(o, gain, hg):
    return _rms(o, gain) * _silu(hg)


def _proj_prompt_kernel(x_ref, gain_ref, wqk_hi_ref, wqk_lo_ref, wrest_ref, cos_ref, sin_ref, lb_ref, hgain_ref,
                        q_ref, k_ref, v_ref, kbf_ref, vt_ref, kmean_ref, hout_ref, state_ref, st_ref):
    t = pl.program_id(1)

    @pl.when(t == 0)
    def _():
        st_ref[...] = jnp.zeros_like(st_ref)

    nr, ts, D = x_ref.shape
    cos = jnp.concatenate([cos_ref[...]] * nr, axis=0)
    sin = jnp.concatenate([sin_ref[...]] * nr, axis=0)
    q, k, v, hq, hk, hv, g, hg = _project(x_ref[...].reshape(nr * ts, D), gain_ref[...], wqk_hi_ref[...],
                                          wqk_lo_ref[...], wrest_ref[...], cos, sin, lb_ref[...])
    n_blk = ts // MOBA_BLOCK
    for r in range(nr):
        rows = slice(r * ts, (r + 1) * ts)
        q_ref[r] = q[rows]
        for h in range(N_HEADS):
            k_ref[r, pl.ds(h, ts, stride=N_HEADS), :] = k[rows, h * HEAD_DIM:(h + 1) * HEAD_DIM]
            v_ref[r, pl.ds(h, ts, stride=N_HEADS), :] = v[rows, h * HEAD_DIM:(h + 1) * HEAD_DIM]
        kbf_ref[r] = _bf(k[rows])
        for n in range(n_blk):
            sl = slice(r * ts + n * MOBA_BLOCK, r * ts + (n + 1) * MOBA_BLOCK)
            vt_ref[r, n] = _bf(v[sl, :].T)
            kmean_ref[r, pl.ds(t * n_blk + n, 1), :] = jnp.mean(k[sl, :], axis=0, keepdims=True)

    l_incl, l_start, l_end = _chunk_matrices(ts, HGRN_CHUNK, HGRN_SUB)
    per_row = lambda lmat: jnp.concatenate([_cumsum_rows(g[r * ts:(r + 1) * ts], lmat) for r in range(nr)], axis=0)
    G = per_row(l_incl)
    Bs = per_row(l_start)
    Be = per_row(l_end)
    hgain = hgain_ref[...]
    heads = [slice(h * HEAD_DIM, (h + 1) * HEAD_DIM) for h in range(N_HEADS)]
    states = [[st_ref[r, h] for h in range(N_HEADS)] for r in range(nr)]
    outs = [[[] for _ in range(N_HEADS)] for _ in range(nr)]
    for ci in range(ts // HGRN_CHUNK):
        for r in range(nr):
            rs = slice(r * ts + ci * HGRN_CHUNK, r * ts + (ci + 1) * HGRN_CHUNK)
            for h, hs in enumerate(heads):
                o, states[r][h] = _hgrn_chunk(hq[rs, hs], hk[rs, hs], hv[rs, hs], G[rs, hs], Bs[rs, hs], Be[rs, hs],
                                              states[r][h], HGRN_SUB, True)
                outs[r][h].append(o)
    for r in range(nr):
        for h, hs in enumerate(heads):
            st_ref[r, h] = states[r][h]
            hout_ref[r, :, hs] = _bf(_hgrn_out(jnp.concatenate(outs[r][h], axis=0), hgain, hg[r * ts:(r + 1) * ts, hs]))

    @pl.when(t == pl.num_programs(1) - 1)
    def _():
        for r in range(nr):
            for h in range(N_HEADS):
                state_ref[r, h] = st_ref[r, h].T


def _proj_sample_kernel(x_ref, gain_ref, wqk_hi_ref, wqk_lo_ref, wrest_ref, cos_ref, sin_ref, lb_ref, hgain_ref,
                        s0_ref, q_ref, k_ref, v_ref, hout_ref, state_ref, *, seq_len):
    rows = x_ref.shape[0]
    q, k, v, hq, hk, hv, g, hg = _project(x_ref[...], gain_ref[...], wqk_hi_ref[...], wqk_lo_ref[...], wrest_ref[...],
                                          cos_ref[...], sin_ref[...], lb_ref[...])
    q_ref[...] = q
    for h in range(N_HEADS):
        k_ref[pl.ds(h, rows, stride=N_HEADS), :] = k[:, h * HEAD_DIM:(h + 1) * HEAD_DIM]
        v_ref[pl.ds(h, rows, stride=N_HEADS), :] = v[:, h * HEAD_DIM:(h + 1) * HEAD_DIM]
    l_incl, _, _ = _chunk_matrices(rows, seq_len, seq_len)
    G = _cumsum_rows(g, l_incl)
    zero = jnp.zeros((seq_len, HEAD_DIM), F32)
    hgain = hgain_ref[...]
    for h in range(N_HEADS):
        hs = slice(h * HEAD_DIM, (h + 1) * HEAD_DIM)
        outs = []
        for si in range(rows // seq_len):
            rs = slice(si * seq_len, (si + 1) * seq_len)
            o, S = _hgrn_chunk(hq[rs, hs], hk[rs, hs], hv[rs, hs], G[rs, hs], zero, zero, s0_ref[si, h], seq_len, False)
            state_ref[si, h] = S
            outs.append(o)
        hout_ref[:, hs] = _bf(_hgrn_out(jnp.concatenate(outs, axis=0), hgain, hg[:, hs]))


def _rotary_tables(pos):
    half = HEAD_DIM // 2
    inv_freq = ROPE_THETA ** (-jnp.arange(half, dtype=F32) / half)
    ang = pos.astype(F32)[:, None] * inv_freq[None, :]
    cos = jnp.cos(ang)
    sin = jnp.sin(ang)
    return jnp.concatenate([cos, cos], axis=1), jnp.concatenate([-sin, sin], axis=1)


def _full(shape):
    return pl.BlockSpec(shape, lambda *_: (0,) * len(shape))


def _proj_prompt(x, gain, wqk_hi, wqk_lo, w_rest, lb, hgain, ts):
    B, S, D = x.shape
    nb = S // MOBA_BLOCK
    cos, sin = _rotary_tables(jnp.arange(S))
    out_shape = (
        jax.ShapeDtypeStruct((B, S, MIX_HALF), F32),
        jax.ShapeDtypeStruct((B, S * N_HEADS, HEAD_DIM), F32),
        jax.ShapeDtypeStruct((B, S * N_HEADS, HEAD_DIM), F32),
        jax.ShapeDtypeStruct((B, S, MIX_HALF), BF16),
        jax.ShapeDtypeStruct((B, nb, MIX_HALF, MOBA_BLOCK), BF16),
        jax.ShapeDtypeStruct((B, nb, MIX_HALF), F32),
        jax.ShapeDtypeStruct((B, S, MIX_HALF), BF16),
        jax.ShapeDtypeStruct((B, N_HEADS, HEAD_DIM, HEAD_DIM), F32),
    )
    nr = 2 if B % 2 == 0 else 1
    tok = pl.BlockSpec((nr, ts, MIX_HALF), lambda b, t: (b, t, 0))
    tok_heads = pl.BlockSpec((nr, ts * N_HEADS, HEAD_DIM), lambda b, t: (b, t, 0))
    return pl.pallas_call(
        _proj_prompt_kernel,
        grid=(B // nr, S // ts),
        in_specs=[
            pl.BlockSpec((nr, ts, D), lambda b, t: (b, t, 0)),
            _full((1, D)), _full(wqk_hi.shape), _full(wqk_lo.shape), _full(w_rest.shape),
            pl.BlockSpec((ts, HEAD_DIM), lambda b, t: (t, 0)),
            pl.BlockSpec((ts, HEAD_DIM), lambda b, t: (t, 0)),
            _full((1, MIX_HALF)), _full((1, HEAD_DIM)),
        ],
        out_specs=(
            tok, tok_heads, tok_heads, tok,
            pl.BlockSpec((nr, ts // MOBA_BLOCK, MIX_HALF, MOBA_BLOCK), lambda b, t: (b, t, 0, 0)),
            pl.BlockSpec((nr, nb, MIX_HALF), lambda b, t: (b, 0, 0)),
            tok,
            pl.BlockSpec((nr, N_HEADS, HEAD_DIM, HEAD_DIM), lambda b, t: (b, 0, 0, 0)),
        ),
        out_shape=out_shape,
        scratch_shapes=[pltpu.VMEM((nr, N_HEADS, HEAD_DIM, HEAD_DIM), F32)],
        compiler_params=pltpu.CompilerParams(dimension_semantics=("arbitrary", "arbitrary"),
                                             vmem_limit_bytes=V7X_VMEM_LIMIT),
        name="proj_prompt",
    )(x, gain, wqk_hi, wqk_lo, w_rest, cos, sin, lb, hgain)


def _proj_sample(x, gain, wqk_hi, wqk_lo, w_rest, lb, hgain, state0, past_len, seqs_per_tile):
    DB, T, D = x.shape
    rows = seqs_per_tile * T
    cos, sin = _rotary_tables(past_len + jnp.arange(T))
    cos = jnp.tile(cos, (seqs_per_tile, 1))
    sin = jnp.tile(sin, (seqs_per_tile, 1))
    out_shape = (
        jax.ShapeDtypeStruct((DB * T, MIX_HALF), F32),
        jax.ShapeDtypeStruct((DB * T * N_HEADS, HEAD_DIM), F32),
        jax.ShapeDtypeStruct((DB * T * N_HEADS, HEAD_DIM), F32),
        jax.ShapeDtypeStruct((DB * T, MIX_HALF), BF16),
        jax.ShapeDtypeStruct((DB, N_HEADS, HEAD_DIM, HEAD_DIM), F32),
    )
    tok = pl.BlockSpec((rows, MIX_HALF), lambda i: (i, 0))
    tok_heads = pl.BlockSpec((rows * N_HEADS, HEAD_DIM), lambda i: (i, 0))
    st = pl.BlockSpec((seqs_per_tile, N_HEADS, HEAD_DIM, HEAD_DIM), lambda i: (i, 0, 0, 0))
    return pl.pallas_call(
        functools.partial(_proj_sample_kernel, seq_len=T),
        grid=(DB // seqs_per_tile,),
        in_specs=[
            pl.BlockSpec((rows, D), lambda i: (i, 0)),
            _full((1, D)), _full(wqk_hi.shape), _full(wqk_lo.shape), _full(w_rest.shape),
            _full((rows, HEAD_DIM)), _full((rows, HEAD_DIM)),
            _full((1, MIX_HALF)), _full((1, HEAD_DIM)),
            st,
        ],
        out_specs=(tok, tok_heads, tok_heads, tok, st),
        out_shape=out_shape,
        compiler_params=pltpu.CompilerParams(dimension_semantics=("arbitrary",),
                                             vmem_limit_bytes=V7X_VMEM_LIMIT),
        name="proj_sample",
    )(x.reshape(DB * T, D), gain, wqk_hi, wqk_lo, w_rest, cos, sin, lb, hgain, state0)


def _top_blocks(gate, n_valid, k_top, axis):
    nb = gate.shape[axis]
    blk = lax.broadcasted_iota(jnp.int32, gate.shape, axis)
    neg = jnp.float32(-jnp.inf)
    g = jnp.where(blk < n_valid, gate, neg)
    sel = jnp.zeros(gate.shape, F32)
    for _ in range(k_top):
        m = jnp.max(g, axis=axis, keepdims=True)
        first = jnp.min(jnp.where(g == m, blk, nb), axis=axis, keepdims=True)
        pick = blk == first
        sel = jnp.where(pick & (m > neg), 1.0, sel)
        g = jnp.where(pick, neg, g)
    return sel


def _moba_prompt_kernel(q_ref, kbf_ref, vt_ref, kmean_ref, o_ref, sel_ref, sa_ref, sb_ref, *, k_top, group, heads):
    i = pl.program_id(2)
    neg = jnp.float32(-jnp.inf)
    start = pl.multiple_of(i * MOBA_BLOCK, MOBA_BLOCK)
    lanes = [slice(hh * HEAD_DIM, (hh + 1) * HEAD_DIM) for hh in range(heads)]
    rows = group * MOBA_BLOCK
    last_group = kbf_ref.shape[1] // rows - 1

    qts, ms, ls, accs = [], [], [], []
    for hh, hl in enumerate(lanes):
        q_t = q_ref[0, :, hl].T
        chosen = _top_blocks(_mm3(kmean_ref[0, :, hl], q_t), i, k_top, 0)
        sel_ref[hh] = jnp.where(chosen > 0.0, 0.0, neg)
        qt = _bf(q_t * (HEAD_DIM ** -0.5 * LOG2_E))
        s = jnp.dot(kbf_ref[0, pl.ds(start, MOBA_BLOCK), hl], qt, preferred_element_type=F32)
        krow = lax.broadcasted_iota(jnp.int32, s.shape, 0)
        qcol = lax.broadcasted_iota(jnp.int32, s.shape, 1)
        s = jnp.where(krow <= qcol, s, neg)
        m = jnp.max(s, axis=0, keepdims=True)
        p = jnp.exp2(s - m)
        qts.append(qt)
        ms.append(m)
        ls.append(jnp.sum(p, axis=0, keepdims=True))
        accs.append(jnp.dot(vt_ref[0, i, hl, :], _bf(p), preferred_element_type=F32))

    def score(g, buf, hh):
        g = jnp.minimum(g, last_group)
        st = pl.multiple_of(g * rows, rows)
        s = jnp.dot(kbf_ref[0, pl.ds(st, rows), lanes[hh]], qts[hh], preferred_element_type=F32)
        mg = None
        for u in range(group):
            su = s[u * MOBA_BLOCK:(u + 1) * MOBA_BLOCK, :] + sel_ref[hh, pl.ds(g * group + u, 1), :]
            buf[hh, u * MOBA_BLOCK:(u + 1) * MOBA_BLOCK, :] = su
            mu = jnp.max(su, axis=0, keepdims=True)
            mg = mu if mg is None else jnp.maximum(mg, mu)
        return mg

    def absorb(g, buf, hh, mg, m, l, acc):
        m_new = jnp.maximum(m, mg)
        alpha = jnp.exp2(m - m_new)
        l = alpha * l
        acc = alpha * acc
        for u in range(group):
            p = jnp.exp2(buf[hh, u * MOBA_BLOCK:(u + 1) * MOBA_BLOCK, :] - m_new)
            l = l + jnp.sum(p, axis=0, keepdims=True)
            acc = acc + jnp.dot(vt_ref[0, g * group + u, lanes[hh], :], _bf(p), preferred_element_type=F32)
        return m_new, l, acc

    def body(k, carry):
        state, m_a = carry
        m_b = [score(2 * k + 1, sb_ref, hh) for hh in range(heads)]
        state = [absorb(2 * k, sa_ref, hh, m_a[hh], *state[hh]) for hh in range(heads)]
        m_a = [score(2 * k + 2, sa_ref, hh) for hh in range(heads)]
        state = [absorb(2 * k + 1, sb_ref, hh, m_b[hh], *state[hh]) for hh in range(heads)]
        return state, m_a

    n_groups = (i + group - 1) // group
    first = [score(0, sa_ref, hh) for hh in range(heads)]
    state, _ = lax.fori_loop(0, (n_groups + 1) // 2, body, ([(ms[hh], ls[hh], accs[hh]) for hh in range(heads)], first))
    for hh, hl in enumerate(lanes):
        _, l, acc = state[hh]
        o_ref[0, :, hl] = _bf((acc / l).T)


def _moba_prompt(q, kbf, vt, kmean):
    B, S, _ = q.shape
    nb = S // MOBA_BLOCK
    k_top = min(MOBA_TOPK, nb - 1)
    group = 2 if nb % 4 == 0 else 1
    assert nb % (2 * group) == 0, "past blocks are absorbed two groups per loop trip"
    heads = 4
    width = heads * HEAD_DIM
    return pl.pallas_call(
        functools.partial(_moba_prompt_kernel, k_top=k_top, group=group, heads=heads),
        grid=(B, N_HEADS // heads, nb),
        in_specs=[
            pl.BlockSpec((1, MOBA_BLOCK, width), lambda b, h, i: (b, i, h)),
            pl.BlockSpec((1, S, width), lambda b, h, i: (b, 0, h)),
            pl.BlockSpec((1, nb, width, MOBA_BLOCK), lambda b, h, i: (b, 0, h, 0)),
            pl.BlockSpec((1, nb, width), lambda b, h, i: (b, 0, h)),
        ],
        out_specs=pl.BlockSpec((1, MOBA_BLOCK, width), lambda b, h, i: (b, i, h)),
        out_shape=jax.ShapeDtypeStruct((B, S, MIX_HALF), BF16),
        scratch_shapes=[pltpu.VMEM((heads, nb, MOBA_BLOCK), F32),
                        pltpu.VMEM((heads, group * MOBA_BLOCK, MOBA_BLOCK), F32),
                        pltpu.VMEM((heads, group * MOBA_BLOCK, MOBA_BLOCK), F32)],
        compiler_params=pltpu.CompilerParams(dimension_semantics=("arbitrary", "arbitrary", "arbitrary"),
                                             vmem_limit_bytes=V7X_VMEM_LIMIT),
        name="moba_prompt",
    )(q, kbf, vt, kmean)


def _heads_to_lanes(ref):
    tokens = ref.shape[0] // N_HEADS
    return jnp.concatenate([ref[pl.ds(h, tokens, stride=N_HEADS), :] for h in range(N_HEADS)], axis=1)


def _moba_sample_scores_kernel(pt_ref, q_ref, kn_ref, vn_ref, expand_ref, *refs, pages_per_step, n_steps, k_top):
    kp_refs = refs[:pages_per_step]
    p_ref, acc0_ref, linv_ref = refs[pages_per_step:pages_per_step + 3]
    qbd_ref, s_ref, ksum_ref = refs[pages_per_step + 3:]
    del pt_ref
    step = pl.program_id(1)
    T = q_ref.shape[1]
    npair = N_HEADS * T
    n_pages = s_ref.shape[0]
    pages_per_blk = MOBA_BLOCK // PAGE_SIZE
    blks_per_step = pages_per_step // pages_per_blk
    neg = jnp.float32(-jnp.inf)

    @pl.when(step == 0)
    def _():
        pair_head = lax.broadcasted_iota(jnp.int32, (npair, MIX_HALF), 0) // T
        lane_head = lax.broadcasted_iota(jnp.int32, (npair, MIX_HALF), 1) // HEAD_DIM
        q4 = jnp.concatenate([q_ref[0]] * N_HEADS, axis=0)
        qbd_ref[...] = jnp.where(pair_head == lane_head, q4, 0.0)

    qs = _bf(qbd_ref[...] * (HEAD_DIM ** -0.5))
    for n in range(blks_per_step):
        ksum = jnp.zeros((1, MIX_HALF), F32)
        for pp in range(pages_per_blk):
            p = n * pages_per_blk + pp
            kp = _heads_to_lanes(kp_refs[p])
            s_ref[step * pages_per_step + p] = _mm_nt(qs, kp)
            ksum = ksum + jnp.sum(kp, axis=0, keepdims=True)
        ksum_ref[pl.ds(step * blks_per_step + n, 1), :] = ksum

    @pl.when(step == n_steps - 1)
    def _():
        n_blocks = ksum_ref.shape[0]
        kmean = ksum_ref[...] * (1.0 / MOBA_BLOCK)
        sel = _top_blocks(_mm3_nt(kmean, qbd_ref[...]), n_blocks, k_top, 0)
        chosen = _mm_tn(sel, expand_ref[...])
        s_own = _mm_nt(qs, _heads_to_lanes(kn_ref))
        key_t = lax.broadcasted_iota(jnp.int32, (npair, T), 1)
        qry_t = lax.broadcasted_iota(jnp.int32, (npair, T), 0) % T
        s_own = jnp.where(key_t <= qry_t, s_own, neg)

        def masked(p):
            return jnp.where(chosen[:, p * PAGE_SIZE:(p + 1) * PAGE_SIZE] > 0.5, s_ref[p], neg)

        m_lanes = masked(0)
        for p in range(1, n_pages):
            m_lanes = jnp.maximum(m_lanes, masked(p))
        m = jnp.maximum(jnp.max(m_lanes, axis=1, keepdims=True), jnp.max(s_own, axis=1, keepdims=True))
        l_lanes = jnp.zeros((npair, PAGE_SIZE), F32)
        for p in range(n_pages):
            w = jnp.exp(masked(p) - m)
            l_lanes = l_lanes + w
            p_ref[0, p] = _bf(w)
        w_own = jnp.exp(s_own - m)
        l = jnp.sum(l_lanes, axis=1, keepdims=True) + jnp.sum(w_own, axis=1, keepdims=True)
        acc0_ref[0] = _mm(w_own, _heads_to_lanes(vn_ref))
        linv_ref[0] = jnp.broadcast_to(1.0 / l, (npair, HEAD_DIM))


def _moba_sample_values_kernel(pt_ref, p_ref, acc0_ref, linv_ref, *refs, pages_per_step, n_steps):
    vp_refs = refs[:pages_per_step]
    o_ref = refs[pages_per_step]
    acc_ref = refs[pages_per_step + 1]
    del pt_ref
    step = pl.program_id(1)
    T = o_ref.shape[1]

    @pl.when(step == 0)
    def _():
        acc_ref[...] = acc0_ref[0]

    acc = acc_ref[...]
    for p in range(pages_per_step):
        vp = _bf(_heads_to_lanes(vp_refs[p]))
        acc = acc + jnp.dot(p_ref[0, step * pages_per_step + p], vp, preferred_element_type=F32)
    acc_ref[...] = acc

    @pl.when(step == n_steps - 1)
    def _():
        linv = linv_ref[0]
        o_ref[0] = jnp.concatenate(
            [acc[h * T:(h + 1) * T, h * HEAD_DIM:(h + 1) * HEAD_DIM] * linv[h * T:(h + 1) * T, :] for h in range(N_HEADS)],
            axis=1)


def _moba_sample(q, k_new, v_new, cache_k, cache_v, page_table, layer, past_len, pages_per_step):
    DB, T, _ = q.shape
    n_pages = page_table.shape[1]
    pages_per_blk = MOBA_BLOCK // PAGE_SIZE
    n_full = past_len // MOBA_BLOCK
    assert n_pages == n_full * pages_per_blk, "the current block must hold no cached rows"
    assert n_pages % pages_per_step == 0 and pages_per_step % pages_per_blk == 0
    n_steps = n_pages // pages_per_step
    k_top = min(MOBA_TOPK, n_full)
    npair = N_HEADS * T
    n_keys = n_pages * PAGE_SIZE
    pt = page_table.astype(jnp.int32).reshape(DB * n_pages)
    expand = (jnp.arange(n_keys)[None, :] // MOBA_BLOCK == jnp.arange(n_full)[:, None]).astype(BF16)

    def page_map(p):
        return lambda b, s, pt_ref: (layer * n_pool + pt_ref[b * n_pages + s * pages_per_step + p], 0)

    per_seq = lambda *tail: (lambda b, s, pt_ref: (b,) + tail)
    page = (PAGE_SIZE * N_HEADS, HEAD_DIM)
    n_pool = cache_k.shape[1]
    cache_k = cache_k.reshape(-1, HEAD_DIM)
    cache_v = cache_v.reshape(-1, HEAD_DIM)
    new_tok = pl.BlockSpec((T * N_HEADS, HEAD_DIM), per_seq(0))
    p_spec = pl.BlockSpec((1, n_pages, npair, PAGE_SIZE), per_seq(0, 0, 0))
    acc_spec = pl.BlockSpec((1, npair, MIX_HALF), per_seq(0, 0))
    linv_spec = pl.BlockSpec((1, npair, HEAD_DIM), per_seq(0, 0))
    params = pltpu.CompilerParams(dimension_semantics=("arbitrary", "arbitrary"), vmem_limit_bytes=V7X_VMEM_LIMIT)

    weights, acc0, linv = pl.pallas_call(
        functools.partial(_moba_sample_scores_kernel, pages_per_step=pages_per_step, n_steps=n_steps, k_top=k_top),
        grid_spec=pltpu.PrefetchScalarGridSpec(
            num_scalar_prefetch=1,
            grid=(DB, n_steps),
            in_specs=[pl.BlockSpec((1, T, MIX_HALF), per_seq(0, 0)), new_tok, new_tok,
                      pl.BlockSpec((n_full, n_keys), lambda b, s, pt_ref: (0, 0))]
            + [pl.BlockSpec(page, page_map(p)) for p in range(pages_per_step)],
            out_specs=(p_spec, acc_spec, linv_spec),
            scratch_shapes=[
                pltpu.VMEM((npair, MIX_HALF), F32),
                pltpu.VMEM((n_pages, npair, PAGE_SIZE), F32),
                pltpu.VMEM((n_full, MIX_HALF), F32),
            ],
        ),
        out_shape=(jax.ShapeDtypeStruct((DB, n_pages, npair, PAGE_SIZE), BF16),
                   jax.ShapeDtypeStruct((DB, npair, MIX_HALF), F32),
                   jax.ShapeDtypeStruct((DB, npair, HEAD_DIM), F32)),
        compiler_params=params,
        name="moba_sample_scores",
    )(pt, q, k_new, v_new, expand, *([cache_k] * pages_per_step))

    return pl.pallas_call(
        functools.partial(_moba_sample_values_kernel, pages_per_step=pages_per_step, n_steps=n_steps),
        grid_spec=pltpu.PrefetchScalarGridSpec(
            num_scalar_prefetch=1,
            grid=(DB, n_steps),
            in_specs=[p_spec, acc_spec, linv_spec] + [pl.BlockSpec(page, page_map(p)) for p in range(pages_per_step)],
            out_specs=pl.BlockSpec((1, T, MIX_HALF), per_seq(0, 0)),
            scratch_shapes=[pltpu.VMEM((npair, MIX_HALF), F32)],
        ),
        out_shape=jax.ShapeDtypeStruct((DB, T, MIX_HALF), F32),
        compiler_params=params,
        name="moba_sample_values",
    )(pt, weights, acc0, linv, *([cache_v] * pages_per_step))


def _memory_kv_kernel(mem_ref, gain_ref, wk_ref, wv_ref, k_ref, v_ref):
    mn = _bf(_rms(mem_ref[0], gain_ref[...]))
    k = jnp.dot(mn, wk_ref[...], preferred_element_type=F32)
    v = jnp.dot(mn, wv_ref[...], preferred_element_type=F32)
    dh = k_ref.shape[-1]
    for h in range(XA_HEADS):
        k_ref[0, :, h, :] = k[:, h * dh:(h + 1) * dh]
        v_ref[0, :, h, :] = v[:, h * dh:(h + 1) * dh]


def _memory_kv(mem, gain, w_ck, w_cv):
    B, M, D = mem.shape
    dh = w_ck.shape[1] // XA_HEADS
    blk = pl.BlockSpec((1, M, D), lambda b: (b, 0, 0))
    out = jax.ShapeDtypeStruct((B, M, XA_HEADS, dh), F32)
    oblk = pl.BlockSpec((1, M, XA_HEADS, dh), lambda b: (b, 0, 0, 0))
    return pl.pallas_call(
        _memory_kv_kernel,
        grid=(B,),
        in_specs=[blk, _full((1, D)), _full(w_ck.shape), _full(w_cv.shape)],
        out_specs=(oblk, oblk),
        out_shape=(out, out),
        compiler_params=pltpu.CompilerParams(dimension_semantics=("arbitrary",), vmem_limit_bytes=V7X_VMEM_LIMIT),
        name="memory_kv",
    )(mem, gain, w_ck, w_cv)


def _mix_kernel(x_ref, att_ref, hout_ref, wo_ref, gpost_ref, gpre_ref, wq_ref, h_ref, q_ref):
    half = att_ref.shape[-1]
    mix = (jnp.dot(_bf(att_ref[...]), wo_ref[0:half, :], preferred_element_type=F32)
           + jnp.dot(hout_ref[...], wo_ref[half:, :], preferred_element_type=F32))
    h = x_ref[...] + _rms(mix, gpost_ref[...])
    h_ref[...] = h
    q = jnp.dot(_bf(_rms(h, gpre_ref[...])), wq_ref[...], preferred_element_type=F32)
    q_ref[...] = _bf(q * (q.shape[-1] // XA_HEADS) ** -0.5)


def _mix(x, att, hout, w_out, g_post, g_pre_x, w_cq, tm):
    N, D = x.shape
    half = att.shape[1]
    return pl.pallas_call(
        _mix_kernel,
        grid=(N // tm,),
        in_specs=[
            pl.BlockSpec((tm, D), lambda i: (i, 0)),
            pl.BlockSpec((tm, half), lambda i: (i, 0)),
            pl.BlockSpec((tm, half), lambda i: (i, 0)),
            _full(w_out.shape), _full((1, D)), _full((1, D)), _full(w_cq.shape),
        ],
        out_specs=(pl.BlockSpec((tm, D), lambda i: (i, 0)), pl.BlockSpec((tm, w_cq.shape[1]), lambda i: (i, 0))),
        out_shape=(jax.ShapeDtypeStruct((N, D), F32), jax.ShapeDtypeStruct((N, w_cq.shape[1]), BF16)),
        compiler_params=pltpu.CompilerParams(dimension_semantics=("arbitrary",), vmem_limit_bytes=V7X_VMEM_LIMIT),
        name="mix_out",
    )(x, att, hout, w_out, g_post, g_pre_x, w_cq)


def _xattn_kernel(q_ref, mk_ref, mv_ref, o_ref):
    groups = q_ref.shape[0]
    dh = q_ref.shape[-1] // XA_HEADS
    pieces = dh // HEAD_DIM
    stride = pieces * XA_HEADS
    M = mk_ref.shape[1] // stride

    def head(ref, gi, h):
        return jnp.concatenate([ref[gi, pl.ds(c * XA_HEADS + h, M, stride=stride), :] for c in range(pieces)], axis=1)

    for gi in range(groups):
        outs = []
        for h in range(XA_HEADS):
            s = _mm_nt(q_ref[gi, :, h * dh:(h + 1) * dh], head(mk_ref, gi, h))
            p = jnp.exp(s - jnp.max(s, axis=-1, keepdims=True))
            l = jnp.sum(p, axis=-1, keepdims=True)
            outs.append(_mm(p, head(mv_ref, gi, h)) / l)
        o_ref[gi] = _bf(jnp.concatenate(outs, axis=1))


def _lane_rows(mem):
    NG, M, H, dh = mem.shape
    pieces = dh // HEAD_DIM
    return mem.reshape(NG, M, H, pieces, HEAD_DIM).transpose(0, 1, 3, 2, 4).reshape(NG, M * pieces * H, HEAD_DIM)


def _xattn(q, mk, mv, groups_per_tile, rows_per_tile):
    NG, R, W = q.shape
    mk = _lane_rows(mk)
    mv = _lane_rows(mv)
    mem_blk = (groups_per_tile,) + mk.shape[1:]
    return pl.pallas_call(
        _xattn_kernel,
        grid=(NG // groups_per_tile, R // rows_per_tile),
        in_specs=[
            pl.BlockSpec((groups_per_tile, rows_per_tile, W), lambda g, r: (g, r, 0)),
            pl.BlockSpec(mem_blk, lambda g, r: (g, 0, 0)),
            pl.BlockSpec(mem_blk, lambda g, r: (g, 0, 0)),
        ],
        out_specs=pl.BlockSpec((groups_per_tile, rows_per_tile, W), lambda g, r: (g, r, 0)),
        out_shape=jax.ShapeDtypeStruct((NG, R, W), BF16),
        compiler_params=pltpu.CompilerParams(dimension_semantics=("arbitrary", "arbitrary"),
                                             vmem_limit_bytes=V7X_VMEM_LIMIT),
        name="cross_attn",
    )(q, mk, mv)


def _ffn_kernel(h_ref, o_ref, wco_ref, gpx_ref, gpf_ref, wup_ref, wdn_ref, gpo_ref, y_ref, *, ff_chunk):
    h = h_ref[...] + _rms(jnp.dot(o_ref[...], wco_ref[...], preferred_element_type=F32), gpx_ref[...])
    hn = _bf(_rms(h, gpf_ref[...]))
    acc = jnp.zeros(h.shape, F32)
    for c in range(wup_ref.shape[1] // ff_chunk):
        cs = slice(c * ff_chunk, (c + 1) * ff_chunk)
        up = jnp.maximum(jnp.dot(hn, wup_ref[:, cs], preferred_element_type=F32), 0.0)
        acc = acc + jnp.dot(_bf(up * up), wdn_ref[cs, :], preferred_element_type=F32)
    y_ref[...] = h + _rms(acc, gpo_ref[...])


def _ffn(h, o, w_co, g_post_x, g_pre_ffn, w_up, w_down, g_post_ffn, tm, ff_chunk):
    N, D = h.shape
    tok = pl.BlockSpec((tm, D), lambda i: (i, 0))
    return pl.pallas_call(
        functools.partial(_ffn_kernel, ff_chunk=ff_chunk),
        grid=(N // tm,),
        in_specs=[tok, pl.BlockSpec((tm, o.shape[1]), lambda i: (i, 0)),
                  _full(w_co.shape), _full((1, D)), _full((1, D)), _full(w_up.shape), _full(w_down.shape),
                  _full((1, D))],
        out_specs=tok,
        out_shape=jax.ShapeDtypeStruct((N, D), F32),
        compiler_params=pltpu.CompilerParams(dimension_semantics=("arbitrary",), vmem_limit_bytes=V7X_VMEM_LIMIT),
        name="co_ffn",
    )(h, o, w_co, g_post_x, g_pre_ffn, w_up, w_down, g_post_ffn)


def _tiles(n_prompt_seq):
    ts = MOBA_BLOCK
    tm = min(512, n_prompt_seq)
    return ts, tm


def kernel(x_prompt, x_sample, cache_k, cache_v, state_hgrn, cache_mem_k, cache_mem_v, page_table, mem_prompt, norm_pre_mix, w_in, hgrn_lb, hgrn_norm, w_out, norm_post_mix, norm_mem, norm_pre_x, w_cq, w_ck, w_cv, w_co, norm_post_x, norm_pre_ffn, w_up, w_down, norm_post_ffn):
    B, S, D = x_prompt.shape
    DB, T, _ = x_sample.shape
    depth = w_in.shape[0]
    past_len = page_table.shape[1] * PAGE_SIZE
    assert D == 2 * MIX_HALF and S % MOBA_BLOCK == 0 and cache_k.shape[2] == PAGE_SIZE
    ts, tm = _tiles(S)
    tm_s = min(tm, DB * T)
    seqs_per_tile = min(16, DB)
    xa_w = w_cq.shape[2]

    lb_all = jnp.cumsum(jax.nn.softmax(hgrn_lb.astype(F32), axis=0), axis=0)
    row = lambda a: a.reshape(1, -1)

    yp, ys = x_prompt, x_sample
    kp_l, vp_l, sp_l, mkp_l, mvp_l, ks_l, vs_l, ss_l = [], [], [], [], [], [], [], []
    for l in range(depth):
        w_qk = w_in[l][:, :2 * MIX_HALF]
        wqk_hi = _bf(w_qk)
        wqk_lo = _bf(w_qk - wqk_hi.astype(F32))
        w_rest = _bf(w_in[l][:, 2 * MIX_HALF:])
        lb = row(lb_all[l])
        hgain = row(hgrn_norm[l])
        wo_b, wcq_b, wck_b, wcv_b, wco_b = _bf(w_out[l]), _bf(w_cq[l]), _bf(w_ck[l]), _bf(w_cv[l]), _bf(w_co[l])
        wup_b, wdn_b = _bf(w_up[l]), _bf(w_down[l])

        def trunk_tail(x2d, att2d, hout2d, q_groups, mk, mv, groups_per_tile, rows_per_tile, tile):
            h1, qx = _mix(x2d, att2d, hout2d, wo_b, row(norm_post_mix[l]), row(norm_pre_x[l]), wcq_b, tile)
            ox = _xattn(qx.reshape(q_groups + (xa_w,)), mk, mv, groups_per_tile, rows_per_tile)
            return _ffn(h1, ox.reshape(-1, xa_w), wco_b, row(norm_post_x[l]), row(norm_pre_ffn[l]), wup_b, wdn_b,
                        row(norm_post_ffn[l]), tile, 1024)

        q, k, v, kbf, vt, kmean, hout, s_fin = _proj_prompt(yp, row(norm_pre_mix[l]), wqk_hi, wqk_lo, w_rest, lb, hgain, ts)
        att = _moba_prompt(q, kbf, vt, kmean)
        mkp, mvp = _memory_kv(mem_prompt, row(norm_mem[l]), wck_b, wcv_b)
        yp = trunk_tail(yp.reshape(B * S, D), att.reshape(B * S, MIX_HALF), hout.reshape(B * S, MIX_HALF),
                        (B, S), mkp, mvp, 1, tm, tm).reshape(B, S, D)
        kp_l.append(k.reshape(B, S, N_HEADS, HEAD_DIM))
        vp_l.append(v.reshape(B, S, N_HEADS, HEAD_DIM))
        sp_l.append(s_fin)
        mkp_l.append(mkp)
        mvp_l.append(mvp)

        q, k, v, hout, s_new = _proj_sample(ys, row(norm_pre_mix[l]), wqk_hi, wqk_lo, w_rest, lb, hgain,
                                            state_hgrn[l], past_len, seqs_per_tile)
        att = _moba_sample(q.reshape(DB, T, MIX_HALF), k, v, cache_k, cache_v, page_table, l, past_len,
                           min(64, page_table.shape[1]))
        ys = trunk_tail(ys.reshape(DB * T, D), att.reshape(DB * T, MIX_HALF), hout, (DB, T),
                        cache_mem_k[l], cache_mem_v[l], min(4, DB), T, tm_s).reshape(DB, T, D)
        ks_l.append(k.reshape(DB, T, N_HEADS, HEAD_DIM))
        vs_l.append(v.reshape(DB, T, N_HEADS, HEAD_DIM))
        ss_l.append(s_new)
    return (yp, ys, jnp.stack(kp_l), jnp.stack(vp_l), jnp.stack(sp_l), jnp.stack(mkp_l), jnp.stack(mvp_l),
            jnp.stack(ks_l), jnp.stack(vs_l), jnp.stack(ss_l))
```

```python
import functools

import jax
import jax.numpy as jnp
from jax import lax
from jax.experimental import pallas as pl
from jax.experimental.pallas import tpu as pltpu

F32 = jnp.float32
BF16 = jnp.bfloat16

NORM_EPS = 1e-6
ROPE_THETA = 10000.0
HEAD_DIM = 128
N_HEADS = 4
MIX_HALF = N_HEADS * HEAD_DIM
MOBA_BLOCK = 256
MOBA_TOPK = 3
PAGE_SIZE = 128
HGRN_CHUNK = 64
HGRN_SUB = 16
XA_HEADS = 4
LOG2_E = 1.4426950408889634
V7X_VMEM_LIMIT = 56 * 1024 * 1024


def _bf(x):
    return x.astype(BF16)


def _mm(a, b):
    return jnp.dot(_bf(a), _bf(b), preferred_element_type=F32)


def _mm_nt(a, b):
    return lax.dot_general(_bf(a), _bf(b), (((1,), (1,)), ((), ())), preferred_element_type=F32)


def _mm_tn(a, b):
    return lax.dot_general(_bf(a), _bf(b), (((0,), (0,)), ((), ())), preferred_element_type=F32)


def _split2(x):
    hi = x.astype(BF16)
    lo = (x - hi.astype(F32)).astype(BF16)
    return hi, lo


def _split3(x):
    a = x.astype(BF16)
    r = x - a.astype(F32)
    b = r.astype(BF16)
    c = (r - b.astype(F32)).astype(BF16)
    return a, b, c


def _mm3(a, b, transpose_b=False):
    a_hi, a_lo = _split2(a)
    b_hi, b_lo = _split2(b)
    dn = (((1,), (1 if transpose_b else 0,)), ((), ()))
    dot = functools.partial(lax.dot_general, dimension_numbers=dn, preferred_element_type=F32)
    return (dot(a_lo, b_hi) + dot(a_hi, b_lo)) + dot(a_hi, b_hi)


def _mm3_nt(a, b):
    return _mm3(a, b, transpose_b=True)


def _rms(x, gain):
    return x * lax.rsqrt(jnp.mean(x * x, axis=-1, keepdims=True) + NORM_EPS) * gain


def _silu(x):
    return x * (1.0 / (1.0 + jnp.exp(-x)))


def _sigmoid(x):
    return 1.0 / (1.0 + jnp.exp(-x))


def _cumsum_rows(g, lmat):
    g1, g2, g3 = _split3(g)
    dot = functools.partial(jnp.dot, preferred_element_type=F32)
    return (dot(lmat, g3) + dot(lmat, g2)) + dot(lmat, g1)


def _chunk_matrix(rows, chunk):
    r = lax.broadcasted_iota(jnp.int32, (rows, rows), 0)
    c = lax.broadcasted_iota(jnp.int32, (rows, rows), 1)
    return jnp.where(((r // chunk) == (c // chunk)) & (c <= r), 1.0, 0.0).astype(BF16)


def _sub_chunk_bounds(G, chunk, sub):
    width = G.shape[1]
    zero = jnp.zeros((sub, width), F32)
    starts, ends = [], []
    for r0 in range(0, G.shape[0], sub):
        starts.append(zero if r0 % chunk == 0 else jnp.broadcast_to(G[r0 - 1:r0, :], (sub, width)))
        ends.append(jnp.broadcast_to(G[r0 + sub - 1:r0 + sub, :], (sub, width)))
    return jnp.concatenate(starts, axis=0), jnp.concatenate(ends, axis=0)


def _hgrn_chunk(q, k, v, G, Bs, Be, S, sub, value_major):
    c = q.shape[0]
    nsub = c // sub
    qh = q * jnp.exp(G - Bs)
    kd = k * jnp.exp(Bs - G)
    qg = q * jnp.exp(G)
    row = lax.broadcasted_iota(jnp.int32, (c, c), 0)
    col = lax.broadcasted_iota(jnp.int32, (c, c), 1)
    carried = (lambda: _mm_nt(qg, S)) if value_major else (lambda: _mm(qg, S))
    if nsub == 1:
        att = jnp.where(col <= row, _mm_nt(qh, kd), 0.0)
        o = _mm(att, v) + carried()
    else:
        kh = k * jnp.exp(Be - G)
        lhs = [qh]
        z = jnp.zeros((sub, HEAD_DIM), F32)
        rhs_rows = [jnp.concatenate([kd] + [jnp.zeros_like(kd)] * (nsub - 1), axis=1)]
        for j in range(nsub - 1):
            g_end_j = G[(j + 1) * sub - 1:(j + 1) * sub, :]
            lhs.append(q * jnp.exp(jnp.minimum(G - g_end_j, 0.0)))
            parts = [z] * nsub
            parts[j + 1] = kh[j * sub:(j + 1) * sub, :]
            rhs_rows.append(jnp.concatenate(parts, axis=1))
        n_off = (nsub - 1) * sub
        R = _mm_nt(jnp.concatenate(lhs, axis=1), jnp.concatenate(rhs_rows, axis=0))
        width = c + n_off
        row2 = lax.broadcasted_iota(jnp.int32, (c, width), 0)
        col2 = lax.broadcasted_iota(jnp.int32, (c, width), 1)
        diag_ok = (col2 < c) & (col2 <= row2) & ((col2 // sub) == (row2 // sub))
        off_ok = (col2 >= c) & (col2 < c + n_off) & (((col2 - c) // sub) < (row2 // sub))
        att2 = jnp.where(diag_ok | off_ok, R, 0.0)
        v_rows = [v, v[:n_off, :]]
        if value_major:
            o = _mm(att2, jnp.concatenate(v_rows, axis=0)) + carried()
        else:
            o = _mm(jnp.concatenate([att2, qg], axis=1), jnp.concatenate(v_rows + [S], axis=0))
    g_end = G[c - 1:c, :]
    kk = k * jnp.exp(g_end - G)
    if value_major:
        S_new = jnp.exp(g_end) * S + _mm_tn(v, kk)
    else:
        decay = jnp.broadcast_to(jnp.exp(g_end), (HEAD_DIM, HEAD_DIM)).T
        S_new = decay * S + _mm_tn(kk, v)
    return o, S_new


def _rotary(x, cos, sin):
    return x * cos + pltpu.roll(x, HEAD_DIM // 2, 1) * sin


def _project(x, gain, wqk_hi, wqk_lo, w_rest, cos, sin, lb):
    hn = _rms(x, gain)
    hn_hi, hn_lo = _split2(hn)
    dot = functools.partial(jnp.dot, preferred_element_type=F32)
    qk = (dot(hn_lo, wqk_hi) + dot(hn_hi, wqk_lo)) + dot(hn_hi, wqk_hi)
    rest = dot(hn_hi, w_rest)
    q = jnp.concatenate([_rotary(qk[:, h * HEAD_DIM:(h + 1) * HEAD_DIM], cos, sin) for h in range(N_HEADS)], axis=1)
    k = jnp.concatenate([_rotary(qk[:, MIX_HALF + h * HEAD_DIM:MIX_HALF + (h + 1) * HEAD_DIM], cos, sin)
                         for h in range(N_HEADS)], axis=1)
    v = rest[:, 0:MIX_HALF]
    hq = _silu(rest[:, MIX_HALF:2 * MIX_HALF])
    f = lb + (1.0 - lb) * _sigmoid(rest[:, 2 * MIX_HALF:3 * MIX_HALF])
    hk = 1.0 - f
    g = jnp.log(f)
    hv = rest[:, 3 * MIX_HALF:4 * MIX_HALF]
    hg = rest[:, 4 * MIX_HALF:5 * MIX_HALF]
    return q, k, v, hq, hk, hv, g, hg


def _hgrn_out(o, gain, hg):
    return _rms(o, gain) * _silu(hg)


def _proj_prompt_kernel(x_ref, gain_ref, wqk_hi_ref, wqk_lo_ref, wrest_ref, cos_ref, sin_ref, lb_ref, hgain_ref,
                        q_ref, k_ref, v_ref, kbf_ref, vt_ref, kmean_ref, hout_ref, state_ref, st_ref):
    t = pl.program_id(1)

    @pl.when(t == 0)
    def _():
        st_ref[...] = jnp.zeros_like(st_ref)

    nr, ts, D = x_ref.shape
    cos = jnp.concatenate([cos_ref[...]] * nr, axis=0)
    sin = jnp.concatenate([sin_ref[...]] * nr, axis=0)
    q, k, v, hq, hk, hv, g, hg = _project(x_ref[...].reshape(nr * ts, D), gain_ref[...], wqk_hi_ref[...],
                                          wqk_lo_ref[...], wrest_ref[...], cos, sin, lb_ref[...])
    n_blk = ts // MOBA_BLOCK
    for r in range(nr):
        rows = slice(r * ts, (r + 1) * ts)
        q_ref[r] = q[rows]
        for h in range(N_HEADS):
            k_ref[r, pl.ds(h, ts, stride=N_HEADS), :] = k[rows, h * HEAD_DIM:(h + 1) * HEAD_DIM]
            v_ref[r, pl.ds(h, ts, stride=N_HEADS), :] = v[rows, h * HEAD_DIM:(h + 1) * HEAD_DIM]
        kbf_ref[r] = _bf(k[rows])
        for n in range(n_blk):
            sl = slice(r * ts + n * MOBA_BLOCK, r * ts + (n + 1) * MOBA_BLOCK)
            vt_ref[r, n] = _bf(v[sl, :].T)
            kmean_ref[r, pl.ds(t * n_blk + n, 1), :] = jnp.mean(k[sl, :], axis=0, keepdims=True)

    l_incl = _chunk_matrix(ts, HGRN_CHUNK)
    G = jnp.concatenate([_cumsum_rows(g[r * ts:(r + 1) * ts], l_incl) for r in range(nr)], axis=0)
    Bs, Be = _sub_chunk_bounds(G, HGRN_CHUNK, HGRN_SUB)
    hgain = hgain_ref[...]
    heads = [slice(h * HEAD_DIM, (h + 1) * HEAD_DIM) for h in range(N_HEADS)]
    states = [[st_ref[r, h] for h in range(N_HEADS)] for r in range(nr)]
    outs = [[[] for _ in range(N_HEADS)] for _ in range(nr)]
    for ci in range(ts // HGRN_CHUNK):
        for r in range(nr):
            rs = slice(r * ts + ci * HGRN_CHUNK, r * ts + (ci + 1) * HGRN_CHUNK)
            for h, hs in enumerate(heads):
                o, states[r][h] = _hgrn_chunk(hq[rs, hs], hk[rs, hs], hv[rs, hs], G[rs, hs], Bs[rs, hs], Be[rs, hs],
                                              states[r][h], HGRN_SUB, True)
                outs[r][h].append(o)
    for r in range(nr):
        for h, hs in enumerate(heads):
            st_ref[r, h] = states[r][h]
            hout_ref[r, :, hs] = _bf(_hgrn_out(jnp.concatenate(outs[r][h], axis=0), hgain, hg[r * ts:(r + 1) * ts, hs]))

    @pl.when(t == pl.num_programs(1) - 1)
    def _():
        for r in range(nr):
            for h in range(N_HEADS):
                state_ref[r, h] = st_ref[r, h].T


def _proj_sample_kernel(x_ref, gain_ref, wqk_hi_ref, wqk_lo_ref, wrest_ref, cos_ref, sin_ref, lb_ref, hgain_ref,
                        s0_ref, q_ref, k_ref, v_ref, hout_ref, state_ref, *, seq_len):
    rows = x_ref.shape[0]
    q, k, v, hq, hk, hv, g, hg = _project(x_ref[...], gain_ref[...], wqk_hi_ref[...], wqk_lo_ref[...], wrest_ref[...],
                                          cos_ref[...], sin_ref[...], lb_ref[...])
    q_ref[...] = q
    for h in range(N_HEADS):
        k_ref[pl.ds(h, rows, stride=N_HEADS), :] = k[:, h * HEAD_DIM:(h + 1) * HEAD_DIM]
        v_ref[pl.ds(h, rows, stride=N_HEADS), :] = v[:, h * HEAD_DIM:(h + 1) * HEAD_DIM]
    G = _cumsum_rows(g, _chunk_matrix(rows, seq_len))
    zero = jnp.zeros((seq_len, HEAD_DIM), F32)
    hgain = hgain_ref[...]
    for h in range(N_HEADS):
        hs = slice(h * HEAD_DIM, (h + 1) * HEAD_DIM)
        outs = []
        for si in range(rows // seq_len):
            rs = slice(si * seq_len, (si + 1) * seq_len)
            o, S = _hgrn_chunk(hq[rs, hs], hk[rs, hs], hv[rs, hs], G[rs, hs], zero, zero, s0_ref[si, h], seq_len, False)
            state_ref[si, h] = S
            outs.append(o)
        hout_ref[:, hs] = _bf(_hgrn_out(jnp.concatenate(outs, axis=0), hgain, hg[:, hs]))


def _rotary_tables(pos):
    half = HEAD_DIM // 2
    inv_freq = ROPE_THETA ** (-jnp.arange(half, dtype=F32) / half)
    ang = pos.astype(F32)[:, None] * inv_freq[None, :]
    cos = jnp.cos(ang)
    sin = jnp.sin(ang)
    return jnp.concatenate([cos, cos], axis=1), jnp.concatenate([-sin, sin], axis=1)


def _full(shape):
    return pl.BlockSpec(shape, lambda *_: (0,) * len(shape))


def _proj_prompt(x, gain, wqk_hi, wqk_lo, w_rest, lb, hgain, ts):
    B, S, D = x.shape
    nb = S // MOBA_BLOCK
    cos, sin = _rotary_tables(jnp.arange(S))
    out_shape = (
        jax.ShapeDtypeStruct((B, S, MIX_HALF), F32),
        jax.ShapeDtypeStruct((B, S * N_HEADS, HEAD_DIM), F32),
        jax.ShapeDtypeStruct((B, S * N_HEADS, HEAD_DIM), F32),
        jax.ShapeDtypeStruct((B, S, MIX_HALF), BF16),
        jax.ShapeDtypeStruct((B, nb, MIX_HALF, MOBA_BLOCK), BF16),
        jax.ShapeDtypeStruct((B, nb, MIX_HALF), F32),
        jax.ShapeDtypeStruct((B, S, MIX_HALF), BF16),
        jax.ShapeDtypeStruct((B, N_HEADS, HEAD_DIM, HEAD_DIM), F32),
    )
    nr = 2 if B % 2 == 0 else 1
    tok = pl.BlockSpec((nr, ts, MIX_HALF), lambda b, t: (b, t, 0))
    tok_heads = pl.BlockSpec((nr, ts * N_HEADS, HEAD_DIM), lambda b, t: (b, t, 0))
    return pl.pallas_call(
        _proj_prompt_kernel,
        grid=(B // nr, S // ts),
        in_specs=[
            pl.BlockSpec((nr, ts, D), lambda b, t: (b, t, 0)),
            _full((1, D)), _full(wqk_hi.shape), _full(wqk_lo.shape), _full(w_rest.shape),
            pl.BlockSpec((ts, HEAD_DIM), lambda b, t: (t, 0)),
            pl.BlockSpec((ts, HEAD_DIM), lambda b, t: (t, 0)),
            _full((1, MIX_HALF)), _full((1, HEAD_DIM)),
        ],
        out_specs=(
            tok, tok_heads, tok_heads, tok,
            pl.BlockSpec((nr, ts // MOBA_BLOCK, MIX_HALF, MOBA_BLOCK), lambda b, t: (b, t, 0, 0)),
            pl.BlockSpec((nr, nb, MIX_HALF), lambda b, t: (b, 0, 0)),
            tok,
            pl.BlockSpec((nr, N_HEADS, HEAD_DIM, HEAD_DIM), lambda b, t: (b, 0, 0, 0)),
        ),
        out_shape=out_shape,
        scratch_shapes=[pltpu.VMEM((nr, N_HEADS, HEAD_DIM, HEAD_DIM), F32)],
        compiler_params=pltpu.CompilerParams(dimension_semantics=("arbitrary", "arbitrary"),
                                             vmem_limit_bytes=V7X_VMEM_LIMIT),
        name="proj_prompt",
    )(x, gain, wqk_hi, wqk_lo, w_rest, cos, sin, lb, hgain)


def _proj_sample(x, gain, wqk_hi, wqk_lo, w_rest, lb, hgain, state0, past_len, seqs_per_tile):
    DB, T, D = x.shape
    rows = seqs_per_tile * T
    cos, sin = _rotary_tables(past_len + jnp.arange(T))
    cos = jnp.tile(cos, (seqs_per_tile, 1))
    sin = jnp.tile(sin, (seqs_per_tile, 1))
    out_shape = (
        jax.ShapeDtypeStruct((DB * T, MIX_HALF), F32),
        jax.ShapeDtypeStruct((DB * T * N_HEADS, HEAD_DIM), F32),
        jax.ShapeDtypeStruct((DB * T * N_HEADS, HEAD_DIM), F32),
        jax.ShapeDtypeStruct((DB * T, MIX_HALF), BF16),
        jax.ShapeDtypeStruct((DB, N_HEADS, HEAD_DIM, HEAD_DIM), F32),
    )
    tok = pl.BlockSpec((rows, MIX_HALF), lambda i: (i, 0))
    tok_heads = pl.BlockSpec((rows * N_HEADS, HEAD_DIM), lambda i: (i, 0))
    st = pl.BlockSpec((seqs_per_tile, N_HEADS, HEAD_DIM, HEAD_DIM), lambda i: (i, 0, 0, 0))
    return pl.pallas_call(
        functools.partial(_proj_sample_kernel, seq_len=T),
        grid=(DB // seqs_per_tile,),
        in_specs=[
            pl.BlockSpec((rows, D), lambda i: (i, 0)),
            _full((1, D)), _full(wqk_hi.shape), _full(wqk_lo.shape), _full(w_rest.shape),
            _full((rows, HEAD_DIM)), _full((rows, HEAD_DIM)),
            _full((1, MIX_HALF)), _full((1, HEAD_DIM)),
            st,
        ],
        out_specs=(tok, tok_heads, tok_heads, tok, st),
        out_shape=out_shape,
        compiler_params=pltpu.CompilerParams(dimension_semantics=("arbitrary",),
                                             vmem_limit_bytes=V7X_VMEM_LIMIT),
        name="proj_sample",
    )(x.reshape(DB * T, D), gain, wqk_hi, wqk_lo, w_rest, cos, sin, lb, hgain, state0)


def _top_blocks(gate, n_valid, k_top, axis):
    nb = gate.shape[axis]
    blk = lax.broadcasted_iota(jnp.int32, gate.shape, axis)
    neg = jnp.float32(-jnp.inf)
    g = jnp.where(blk < n_valid, gate, neg)
    sel = jnp.zeros(gate.shape, F32)
    for _ in range(k_top):
        m = jnp.max(g, axis=axis, keepdims=True)
        first = jnp.min(jnp.where(g == m, blk, nb), axis=axis, keepdims=True)
        pick = blk == first
        sel = jnp.where(pick & (m > neg), 1.0, sel)
        g = jnp.where(pick, neg, g)
    return sel


def _moba_prompt_kernel(q_ref, kbf_ref, vt_ref, kmean_ref, o_ref, sel_ref, sa_ref, sb_ref, *, k_top, group, heads):
    i = pl.program_id(2)
    neg = jnp.float32(-jnp.inf)
    start = pl.multiple_of(i * MOBA_BLOCK, MOBA_BLOCK)
    lanes = [slice(hh * HEAD_DIM, (hh + 1) * HEAD_DIM) for hh in range(heads)]
    rows = group * MOBA_BLOCK
    last_group = kbf_ref.shape[1] // rows - 1

    qts, ms, ls, accs = [], [], [], []
    for hh, hl in enumerate(lanes):
        q_t = q_ref[0, :, hl].T
        chosen = _top_blocks(_mm3(kmean_ref[0, :, hl], q_t), i, k_top, 0)
        sel_ref[hh] = jnp.where(chosen > 0.0, 0.0, neg)
        qt = _bf(q_t * (HEAD_DIM ** -0.5 * LOG2_E))
        s = jnp.dot(kbf_ref[0, pl.ds(start, MOBA_BLOCK), hl], qt, preferred_element_type=F32)
        krow = lax.broadcasted_iota(jnp.int32, s.shape, 0)
        qcol = lax.broadcasted_iota(jnp.int32, s.shape, 1)
        s = jnp.where(krow <= qcol, s, neg)
        m = jnp.max(s, axis=0, keepdims=True)
        p = jnp.exp2(s - m)
        qts.append(qt)
        ms.append(m)
        ls.append(jnp.sum(p, axis=0, keepdims=True))
        accs.append(jnp.dot(vt_ref[0, i, hl, :], _bf(p), preferred_element_type=F32))

    def score(g, buf, hh):
        g = jnp.minimum(g, last_group)
        st = pl.multiple_of(g * rows, rows)
        s = jnp.dot(kbf_ref[0, pl.ds(st, rows), lanes[hh]], qts[hh], preferred_element_type=F32)
        mg = None
        for u in range(group):
            su = s[u * MOBA_BLOCK:(u + 1) * MOBA_BLOCK, :] + sel_ref[hh, pl.ds(g * group + u, 1), :]
            buf[hh, u * MOBA_BLOCK:(u + 1) * MOBA_BLOCK, :] = su
            mu = jnp.max(su, axis=0, keepdims=True)
            mg = mu if mg is None else jnp.maximum(mg, mu)
        return mg

    def absorb(g, buf, hh, mg, m, l, acc):
        m_new = jnp.maximum(m, mg)
        alpha = jnp.exp2(m - m_new)
        l = alpha * l
        acc = alpha * acc
        for u in range(group):
            p = jnp.exp2(buf[hh, u * MOBA_BLOCK:(u + 1) * MOBA_BLOCK, :] - m_new)
            l = l + jnp.sum(p, axis=0, keepdims=True)
            acc = acc + jnp.dot(vt_ref[0, g * group + u, lanes[hh], :], _bf(p), preferred_element_type=F32)
        return m_new, l, acc

    def body(k, carry):
        state, m_a = carry
        m_b = [score(2 * k + 1, sb_ref, hh) for hh in range(heads)]
        state = [absorb(2 * k, sa_ref, hh, m_a[hh], *state[hh]) for hh in range(heads)]
        m_a = [score(2 * k + 2, sa_ref, hh) for hh in range(heads)]
        state = [absorb(2 * k + 1, sb_ref, hh, m_b[hh], *state[hh]) for hh in range(heads)]
        return state, m_a

    n_groups = (i + group - 1) // group
    first = [score(0, sa_ref, hh) for hh in range(heads)]
    state, _ = lax.fori_loop(0, (n_groups + 1) // 2, body, ([(ms[hh], ls[hh], accs[hh]) for hh in range(heads)], first))
    for hh, hl in enumerate(lanes):
        _, l, acc = state[hh]
        o_ref[0, :, hl] = _bf((acc / l).T)


def _moba_prompt(q, kbf, vt, kmean):
    B, S, _ = q.shape
    nb = S // MOBA_BLOCK
    k_top = min(MOBA_TOPK, nb - 1)
    group = 2 if nb % 4 == 0 else 1
    assert nb % (2 * group) == 0, "past blocks are absorbed two groups per loop trip"
    heads = 4
    width = heads * HEAD_DIM
    return pl.pallas_call(
        functools.partial(_moba_prompt_kernel, k_top=k_top, group=group, heads=heads),
        grid=(B, N_HEADS // heads, nb),
        in_specs=[
            pl.BlockSpec((1, MOBA_BLOCK, width), lambda b, h, i: (b, i, h)),
            pl.BlockSpec((1, S, width), lambda b, h, i: (b, 0, h)),
            pl.BlockSpec((1, nb, width, MOBA_BLOCK), lambda b, h, i: (b, 0, h, 0)),
            pl.BlockSpec((1, nb, width), lambda b, h, i: (b, 0, h)),
        ],
        out_specs=pl.BlockSpec((1, MOBA_BLOCK, width), lambda b, h, i: (b, i, h)),
        out_shape=jax.ShapeDtypeStruct((B, S, MIX_HALF), BF16),
        scratch_shapes=[pltpu.VMEM((heads, nb, MOBA_BLOCK), F32),
                        pltpu.VMEM((heads, group * MOBA_BLOCK, MOBA_BLOCK), F32),
                        pltpu.VMEM((heads, group * MOBA_BLOCK, MOBA_BLOCK), F32)],
        compiler_params=pltpu.CompilerParams(dimension_semantics=("arbitrary", "arbitrary", "arbitrary"),
                                             vmem_limit_bytes=V7X_VMEM_LIMIT),
        name="moba_prompt",
    )(q, kbf, vt, kmean)


def _heads_to_lanes(ref):
    tokens = ref.shape[0] // N_HEADS
    return jnp.concatenate([ref[pl.ds(h, tokens, stride=N_HEADS), :] for h in range(N_HEADS)], axis=1)


def _moba_sample_scores_kernel(pt_ref, q_ref, kn_ref, vn_ref, expand_ref, *refs, pages_per_step, n_steps, k_top):
    kp_refs = refs[:pages_per_step]
    p_ref, acc0_ref, linv_ref = refs[pages_per_step:pages_per_step + 3]
    qbd_ref, s_ref, ksum_ref = refs[pages_per_step + 3:]
    del pt_ref
    step = pl.program_id(1)
    T = q_ref.shape[1]
    npair = N_HEADS * T
    n_pages = s_ref.shape[0]
    pages_per_blk = MOBA_BLOCK // PAGE_SIZE
    blks_per_step = pages_per_step // pages_per_blk
    neg = jnp.float32(-jnp.inf)

    @pl.when(step == 0)
    def _():
        pair_head = lax.broadcasted_iota(jnp.int32, (npair, MIX_HALF), 0) // T
        lane_head = lax.broadcasted_iota(jnp.int32, (npair, MIX_HALF), 1) // HEAD_DIM
        q4 = jnp.concatenate([q_ref[0]] * N_HEADS, axis=0)
        qbd_ref[...] = jnp.where(pair_head == lane_head, q4, 0.0)

    qs = _bf(qbd_ref[...] * (HEAD_DIM ** -0.5))
    for n in range(blks_per_step):
        ksum = jnp.zeros((1, MIX_HALF), F32)
        for pp in range(pages_per_blk):
            p = n * pages_per_blk + pp
            kp = _heads_to_lanes(kp_refs[p])
            s_ref[step * pages_per_step + p] = _mm_nt(qs, kp)
            ksum = ksum + jnp.sum(kp, axis=0, keepdims=True)
        ksum_ref[pl.ds(step * blks_per_step + n, 1), :] = ksum

    @pl.when(step == n_steps - 1)
    def _():
        n_blocks = ksum_ref.shape[0]
        kmean = ksum_ref[...] * (1.0 / MOBA_BLOCK)
        sel = _top_blocks(_mm3_nt(kmean, qbd_ref[...]), n_blocks, k_top, 0)
        chosen = _mm_tn(sel, expand_ref[...])
        s_own = _mm_nt(qs, _heads_to_lanes(kn_ref))
        key_t = lax.broadcasted_iota(jnp.int32, (npair, T), 1)
        qry_t = lax.broadcasted_iota(jnp.int32, (npair, T), 0) % T
        s_own = jnp.where(key_t <= qry_t, s_own, neg)

        def masked(p):
            return jnp.where(chosen[:, p * PAGE_SIZE:(p + 1) * PAGE_SIZE] > 0.5, s_ref[p], neg)

        m_lanes = masked(0)
        for p in range(1, n_pages):
            m_lanes = jnp.maximum(m_lanes, masked(p))
        m = jnp.maximum(jnp.max(m_lanes, axis=1, keepdims=True), jnp.max(s_own, axis=1, keepdims=True))
        l_lanes = jnp.zeros((npair, PAGE_SIZE), F32)
        for p in range(n_pages):
            w = jnp.exp(masked(p) - m)
            l_lanes = l_lanes + w
            p_ref[0, p] = _bf(w)
        w_own = jnp.exp(s_own - m)
        l = jnp.sum(l_lanes, axis=1, keepdims=True) + jnp.sum(w_own, axis=1, keepdims=True)
        acc0_ref[0] = _mm(w_own, _heads_to_lanes(vn_ref))
        linv_ref[0] = jnp.broadcast_to(1.0 / l, (npair, HEAD_DIM))


def _moba_sample_values_kernel(pt_ref, p_ref, acc0_ref, linv_ref, *refs, pages_per_step, n_steps):
    vp_refs = refs[:pages_per_step]
    o_ref = refs[pages_per_step]
    acc_ref = refs[pages_per_step + 1]
    del pt_ref
    step = pl.program_id(1)
    T = o_ref.shape[1]

    @pl.when(step == 0)
    def _():
        acc_ref[...] = acc0_ref[0]

    acc = acc_ref[...]
    for p in range(pages_per_step):
        vp = _bf(_heads_to_lanes(vp_refs[p]))
        acc = acc + jnp.dot(p_ref[0, step * pages_per_step + p], vp, preferred_element_type=F32)
    acc_ref[...] = acc

    @pl.when(step == n_steps - 1)
    def _():
        linv = linv_ref[0]
        o_ref[0] = jnp.concatenate(
            [acc[h * T:(h + 1) * T, h * HEAD_DIM:(h + 1) * HEAD_DIM] * linv[h * T:(h + 1) * T, :] for h in range(N_HEADS)],
            axis=1)


def _moba_sample(q, k_new, v_new, cache_k, cache_v, page_table, layer, past_len, pages_per_step):
    DB, T, _ = q.shape
    n_pages = page_table.shape[1]
    pages_per_blk = MOBA_BLOCK // PAGE_SIZE
    n_full = past_len // MOBA_BLOCK
    assert n_pages == n_full * pages_per_blk, "the current block must hold no cached rows"
    assert n_pages % pages_per_step == 0 and pages_per_step % pages_per_blk == 0
    n_steps = n_pages // pages_per_step
    k_top = min(MOBA_TOPK, n_full)
    npair = N_HEADS * T
    n_keys = n_pages * PAGE_SIZE
    pt = page_table.astype(jnp.int32).reshape(DB * n_pages)
    expand = (jnp.arange(n_keys)[None, :] // MOBA_BLOCK == jnp.arange(n_full)[:, None]).astype(BF16)

    def page_map(p):
        return lambda b, s, pt_ref: (layer * n_pool + pt_ref[b * n_pages + s * pages_per_step + p], 0)

    per_seq = lambda *tail: (lambda b, s, pt_ref: (b,) + tail)
    page = (PAGE_SIZE * N_HEADS, HEAD_DIM)
    n_pool = cache_k.shape[1]
    cache_k = cache_k.reshape(-1, HEAD_DIM)
    cache_v = cache_v.reshape(-1, HEAD_DIM)
    new_tok = pl.BlockSpec((T * N_HEADS, HEAD_DIM), per_seq(0))
    p_spec = pl.BlockSpec((1, n_pages, npair, PAGE_SIZE), per_seq(0, 0, 0))
    acc_spec = pl.BlockSpec((1, npair, MIX_HALF), per_seq(0, 0))
    linv_spec = pl.BlockSpec((1, npair, HEAD_DIM), per_seq(0, 0))
    params = pltpu.CompilerParams(dimension_semantics=("arbitrary", "arbitrary"), vmem_limit_bytes=V7X_VMEM_LIMIT)

    weights, acc0, linv = pl.pallas_call(
        functools.partial(_moba_sample_scores_kernel, pages_per_step=pages_per_step, n_steps=n_steps, k_top=k_top),
        grid_spec=pltpu.PrefetchScalarGridSpec(
            num_scalar_prefetch=1,
            grid=(DB, n_steps),
            in_specs=[pl.BlockSpec((1, T, MIX_HALF), per_seq(0, 0)), new_tok, new_tok,
                      pl.BlockSpec((n_full, n_keys), lambda b, s, pt_ref: (0, 0))]
            + [pl.BlockSpec(page, page_map(p)) for p in range(pages_per_step)],
            out_specs=(p_spec, acc_spec, linv_spec),
            scratch_shapes=[
                pltpu.VMEM((npair, MIX_HALF), F32),
                pltpu.VMEM((n_pages, npair, PAGE_SIZE), F32),
                pltpu.VMEM((n_full, MIX_HALF), F32),
            ],
        ),
        out_shape=(jax.ShapeDtypeStruct((DB, n_pages, npair, PAGE_SIZE), BF16),
                   jax.ShapeDtypeStruct((DB, npair, MIX_HALF), F32),
                   jax.ShapeDtypeStruct((DB, npair, HEAD_DIM), F32)),
        compiler_params=params,
        name="moba_sample_scores",
    )(pt, q, k_new, v_new, expand, *([cache_k] * pages_per_step))

    return pl.pallas_call(
        functools.partial(_moba_sample_values_kernel, pages_per_step=pages_per_step, n_steps=n_steps),
        grid_spec=pltpu.PrefetchScalarGridSpec(
            num_scalar_prefetch=1,
            grid=(DB, n_steps),
            in_specs=[p_spec, acc_spec, linv_spec] + [pl.BlockSpec(page, page_map(p)) for p in range(pages_per_step)],
            out_specs=pl.BlockSpec((1, T, MIX_HALF), per_seq(0, 0)),
            scratch_shapes=[pltpu.VMEM((npair, MIX_HALF), F32)],
        ),
        out_shape=jax.ShapeDtypeStruct((DB, T, MIX_HALF), F32),
        compiler_params=params,
        name="moba_sample_values",
    )(pt, weights, acc0, linv, *([cache_v] * pages_per_step))


def _memory_kv_kernel(mem_ref, gain_ref, wk_ref, wv_ref, k_ref, v_ref):
    mn = _bf(_rms(mem_ref[0], gain_ref[...]))
    k = jnp.dot(mn, wk_ref[...], preferred_element_type=F32)
    v = jnp.dot(mn, wv_ref[...], preferred_element_type=F32)
    dh = k_ref.shape[-1]
    for h in range(XA_HEADS):
        k_ref[0, :, h, :] = k[:, h * dh:(h + 1) * dh]
        v_ref[0, :, h, :] = v[:, h * dh:(h + 1) * dh]


def _memory_kv(mem, gain, w_ck, w_cv):
    B, M, D = mem.shape
    dh = w_ck.shape[1] // XA_HEADS
    blk = pl.BlockSpec((1, M, D), lambda b: (b, 0, 0))
    out = jax.ShapeDtypeStruct((B, M, XA_HEADS, dh), F32)
    oblk = pl.BlockSpec((1, M, XA_HEADS, dh), lambda b: (b, 0, 0, 0))
    return pl.pallas_call(
        _memory_kv_kernel,
        grid=(B,),
        in_specs=[blk, _full((1, D)), _full(w_ck.shape), _full(w_cv.shape)],
        out_specs=(oblk, oblk),
        out_shape=(out, out),
        compiler_params=pltpu.CompilerParams(dimension_semantics=("arbitrary",), vmem_limit_bytes=V7X_VMEM_LIMIT),
        name="memory_kv",
    )(mem, gain, w_ck, w_cv)


def _mix_kernel(x_ref, att_ref, hout_ref, wo_ref, gpost_ref, gpre_ref, wq_ref, h_ref, q_ref):
    half = att_ref.shape[-1]
    mix = (jnp.dot(_bf(att_ref[...]), wo_ref[0:half, :], preferred_element_type=F32)
           + jnp.dot(hout_ref[...], wo_ref[half:, :], preferred_element_type=F32))
    h = x_ref[...] + _rms(mix, gpost_ref[...])
    h_ref[...] = h
    q = jnp.dot(_bf(_rms(h, gpre_ref[...])), wq_ref[...], preferred_element_type=F32)
    q_ref[...] = _bf(q * (q.shape[-1] // XA_HEADS) ** -0.5)


def _mix(x, att, hout, w_out, g_post, g_pre_x, w_cq, tm):
    N, D = x.shape
    half = att.shape[1]
    return pl.pallas_call(
        _mix_kernel,
        grid=(N // tm,),
        in_specs=[
            pl.BlockSpec((tm, D), lambda i: (i, 0)),
            pl.BlockSpec((tm, half), lambda i: (i, 0)),
            pl.BlockSpec((tm, half), lambda i: (i, 0)),
            _full(w_out.shape), _full((1, D)), _full((1, D)), _full(w_cq.shape),
        ],
        out_specs=(pl.BlockSpec((tm, D), lambda i: (i, 0)), pl.BlockSpec((tm, w_cq.shape[1]), lambda i: (i, 0))),
        out_shape=(jax.ShapeDtypeStruct((N, D), F32), jax.ShapeDtypeStruct((N, w_cq.shape[1]), BF16)),
        compiler_params=pltpu.CompilerParams(dimension_semantics=("arbitrary",), vmem_limit_bytes=V7X_VMEM_LIMIT),
        name="mix_out",
    )(x, att, hout, w_out, g_post, g_pre_x, w_cq)


def _xattn_kernel(q_ref, mk_ref, mv_ref, o_ref):
    groups = q_ref.shape[0]
    dh = q_ref.shape[-1] // XA_HEADS
    pieces = dh // HEAD_DIM
    stride = pieces * XA_HEADS
    M = mk_ref.shape[1] // stride

    def head(ref, gi, h):
        return jnp.concatenate([ref[gi, pl.ds(c * XA_HEADS + h, M, stride=stride), :] for c in range(pieces)], axis=1)

    for gi in range(groups):
        outs = []
        for h in range(XA_HEADS):
            s = _mm_nt(q_ref[gi, :, h * dh:(h + 1) * dh], head(mk_ref, gi, h))
            p = jnp.exp(s - jnp.max(s, axis=-1, keepdims=True))
            l = jnp.sum(p, axis=-1, keepdims=True)
            outs.append(_mm(p, head(mv_ref, gi, h)) / l)
        o_ref[gi] = _bf(jnp.concatenate(outs, axis=1))


def _lane_rows(mem):
    NG, M, H, dh = mem.shape
    pieces = dh // HEAD_DIM
    return mem.reshape(NG, M, H, pieces, HEAD_DIM).transpose(0, 1, 3, 2, 4).reshape(NG, M * pieces * H, HEAD_DIM)


def _xattn(q, mk, mv, groups_per_tile, rows_per_tile):
    NG, R, W = q.shape
    mk = _lane_rows(mk)
    mv = _lane_rows(mv)
    mem_blk = (groups_per_tile,) + mk.shape[1:]
    return pl.pallas_call(
        _xattn_kernel,
        grid=(NG // groups_per_tile, R // rows_per_tile),
        in_specs=[
            pl.BlockSpec((groups_per_tile, rows_per_tile, W), lambda g, r: (g, r, 0)),
            pl.BlockSpec(mem_blk, lambda g, r: (g, 0, 0)),
            pl.BlockSpec(mem_blk, lambda g, r: (g, 0, 0)),
        ],
        out_specs=pl.BlockSpec((groups_per_tile, rows_per_tile, W), lambda g, r: (g, r, 0)),
        out_shape=jax.ShapeDtypeStruct((NG, R, W), BF16),
        compiler_params=pltpu.CompilerParams(dimension_semantics=("arbitrary", "arbitrary"),
                                             vmem_limit_bytes=V7X_VMEM_LIMIT),
        name="cross_attn",
    )(q, mk, mv)


def _ffn_kernel(h_ref, o_ref, wco_ref, gpx_ref, gpf_ref, wup_ref, wdn_ref, gpo_ref, y_ref, *, ff_chunk):
    h = h_ref[...] + _rms(jnp.dot(o_ref[...], wco_ref[...], preferred_element_type=F32), gpx_ref[...])
    hn = _bf(_rms(h, gpf_ref[...]))
    acc = jnp.zeros(h.shape, F32)
    for c in range(wup_ref.shape[1] // ff_chunk):
        cs = slice(c * ff_chunk, (c + 1) * ff_chunk)
        up = jnp.maximum(jnp.dot(hn, wup_ref[:, cs], preferred_element_type=F32), 0.0)
        acc = acc + jnp.dot(_bf(up * up), wdn_ref[cs, :], preferred_element_type=F32)
    y_ref[...] = h + _rms(acc, gpo_ref[...])


def _ffn(h, o, w_co, g_post_x, g_pre_ffn, w_up, w_down, g_post_ffn, tm, ff_chunk):
    N, D = h.shape
    tok = pl.BlockSpec((tm, D), lambda i: (i, 0))
    return pl.pallas_call(
        functools.partial(_ffn_kernel, ff_chunk=ff_chunk),
        grid=(N // tm,),
        in_specs=[tok, pl.BlockSpec((tm, o.shape[1]), lambda i: (i, 0)),
                  _full(w_co.shape), _full((1, D)), _full((1, D)), _full(w_up.shape), _full(w_down.shape),
                  _full((1, D))],
        out_specs=tok,
        out_shape=jax.ShapeDtypeStruct((N, D), F32),
        compiler_params=pltpu.CompilerParams(dimension_semantics=("arbitrary",), vmem_limit_bytes=V7X_VMEM_LIMIT),
        name="co_ffn",
    )(h, o, w_co, g_post_x, g_pre_ffn, w_up, w_down, g_post_ffn)


def _tiles(n_prompt_seq):
    ts = MOBA_BLOCK
    tm = min(512, n_prompt_seq)
    return ts, tm


def kernel(x_prompt, x_sample, cache_k, cache_v, state_hgrn, cache_mem_k, cache_mem_v, page_table, mem_prompt, norm_pre_mix, w_in, hgrn_lb, hgrn_norm, w_out, norm_post_mix, norm_mem, norm_pre_x, w_cq, w_ck, w_cv, w_co, norm_post_x, norm_pre_ffn, w_up, w_down, norm_post_ffn):
    B, S, D = x_prompt.shape
    DB, T, _ = x_sample.shape
    depth = w_in.shape[0]
    past_len = page_table.shape[1] * PAGE_SIZE
    assert D == 2 * MIX_HALF and S % MOBA_BLOCK == 0 and cache_k.shape[2] == PAGE_SIZE
    ts, tm = _tiles(S)
    tm_s = min(tm, DB * T)
    seqs_per_tile = min(16, DB)
    xa_w = w_cq.shape[2]

    lb_all = jnp.cumsum(jax.nn.softmax(hgrn_lb.astype(F32), axis=0), axis=0)
    row = lambda a: a.reshape(1, -1)

    yp, ys = x_prompt, x_sample
    kp_l, vp_l, sp_l, mkp_l, mvp_l, ks_l, vs_l, ss_l = [], [], [], [], [], [], [], []
    for l in range(depth):
        w_qk = w_in[l][:, :2 * MIX_HALF]
        wqk_hi = _bf(w_qk)
        wqk_lo = _bf(w_qk - wqk_hi.astype(F32))
        w_rest = _bf(w_in[l][:, 2 * MIX_HALF:])
        lb = row(lb_all[l])
        hgain = row(hgrn_norm[l])
        wo_b, wcq_b, wck_b, wcv_b, wco_b = _bf(w_out[l]), _bf(w_cq[l]), _bf(w_ck[l]), _bf(w_cv[l]), _bf(w_co[l])
        wup_b, wdn_b = _bf(w_up[l]), _bf(w_down[l])

        def trunk_tail(x2d, att2d, hout2d, q_groups, mk, mv, groups_per_tile, rows_per_tile, tile):
            h1, qx = _mix(x2d, att2d, hout2d, wo_b, row(norm_post_mix[l]), row(norm_pre_x[l]), wcq_b, tile)
            ox = _xattn(qx.reshape(q_groups + (xa_w,)), mk, mv, groups_per_tile, rows_per_tile)
            return _ffn(h1, ox.reshape(-1, xa_w), wco_b, row(norm_post_x[l]), row(norm_pre_ffn[l]), wup_b, wdn_b,
                        row(norm_post_ffn[l]), tile, 1024)

        q, k, v, kbf, vt, kmean, hout, s_fin = _proj_prompt(yp, row(norm_pre_mix[l]), wqk_hi, wqk_lo, w_rest, lb, hgain, ts)
        att = _moba_prompt(q, kbf, vt, kmean)
        mkp, mvp = _memory_kv(mem_prompt, row(norm_mem[l]), wck_b, wcv_b)
        yp = trunk_tail(yp.reshape(B * S, D), att.reshape(B * S, MIX_HALF), hout.reshape(B * S, MIX_HALF),
                        (B, S), mkp, mvp, 1, tm, tm).reshape(B, S, D)
        kp_l.append(k.reshape(B, S, N_HEADS, HEAD_DIM))
        vp_l.append(v.reshape(B, S, N_HEADS, HEAD_DIM))
        sp_l.append(s_fin)
        mkp_l.append(mkp)
        mvp_l.append(mvp)

        q, k, v, hout, s_new = _proj_sample(ys, row(norm_pre_mix[l]), wqk_hi, wqk_lo, w_rest, lb, hgain,
                                            state_hgrn[l], past_len, seqs_per_tile)
        att = _moba_sample(q.reshape(DB, T, MIX_HALF), k, v, cache_k, cache_v, page_table, l, past_len,
                           min(64, page_table.shape[1]))
        ys = trunk_tail(ys.reshape(DB * T, D), att.reshape(DB * T, MIX_HALF), hout, (DB, T),
                        cache_mem_k[l], cache_mem_v[l], min(4, DB), T, tm_s).reshape(DB, T, D)
        ks_l.append(k.reshape(DB, T, N_HEADS, HEAD_DIM))
        vs_l.append(v.reshape(DB, T, N_HEADS, HEAD_DIM))
        ss_l.append(s_new)
    return (yp, ys, jnp.stack(kp_l), jnp.stack(vp_l), jnp.stack(sp_l), jnp.stack(mkp_l), jnp.stack(mvp_l),
            jnp.stack(ks_l), jnp.stack(vs_l), jnp.stack(ss_l))
```

```python
import functools
from typing import NamedTuple

import jax
import jax.numpy as jnp
from jax import lax
from jax.experimental import pallas as pl
from jax.experimental.pallas import tpu as pltpu

F32 = jnp.float32
BF16 = jnp.bfloat16

NORM_EPS = 1e-6
ROPE_THETA = 10000.0
HEAD_DIM = 128
N_HEADS = 4
MIX_HALF = N_HEADS * HEAD_DIM
MOBA_BLOCK = 256
MOBA_TOPK = 3
PAGE_SIZE = 128
HGRN_CHUNK = 64
HGRN_SUB = 16
XA_HEADS = 4
LOG2_E = 1.4426950408889634
V7X_VMEM_LIMIT = 56 * 1024 * 1024


def _bf(x):
    return x.astype(BF16)


def _mm(a, b):
    return jnp.dot(_bf(a), _bf(b), preferred_element_type=F32)


def _mm_nt(a, b):
    return lax.dot_general(_bf(a), _bf(b), (((1,), (1,)), ((), ())), preferred_element_type=F32)


def _mm_tn(a, b):
    return lax.dot_general(_bf(a), _bf(b), (((0,), (0,)), ((), ())), preferred_element_type=F32)


def _split2(x):
    hi = x.astype(BF16)
    lo = (x - hi.astype(F32)).astype(BF16)
    return hi, lo


def _split3(x):
    a = x.astype(BF16)
    r = x - a.astype(F32)
    b = r.astype(BF16)
    c = (r - b.astype(F32)).astype(BF16)
    return a, b, c


def _mm3(a, b, transpose_b=False):
    a_hi, a_lo = _split2(a)
    b_hi, b_lo = _split2(b)
    dn = (((1,), (1 if transpose_b else 0,)), ((), ()))
    dot = functools.partial(lax.dot_general, dimension_numbers=dn, preferred_element_type=F32)
    return (dot(a_lo, b_hi) + dot(a_hi, b_lo)) + dot(a_hi, b_hi)


def _mm3_nt(a, b):
    return _mm3(a, b, transpose_b=True)


def _rms(x, gain):
    return x * lax.rsqrt(jnp.mean(x * x, axis=-1, keepdims=True) + NORM_EPS) * gain


def _silu(x):
    return x * (1.0 / (1.0 + jnp.exp(-x)))


def _sigmoid(x):
    return 1.0 / (1.0 + jnp.exp(-x))


def _cumsum_rows(g, lmat):
    g1, g2, g3 = _split3(g)
    dot = functools.partial(jnp.dot, preferred_element_type=F32)
    return (dot(lmat, g3) + dot(lmat, g2)) + dot(lmat, g1)


def _chunk_matrix(rows, chunk):
    r = lax.broadcasted_iota(jnp.int32, (rows, rows), 0)
    c = lax.broadcasted_iota(jnp.int32, (rows, rows), 1)
    return jnp.where(((r // chunk) == (c // chunk)) & (c <= r), 1.0, 0.0).astype(BF16)


def _sub_chunk_bounds(G, chunk, sub):
    width = G.shape[1]
    zero = jnp.zeros((sub, width), F32)
    starts, ends = [], []
    for r0 in range(0, G.shape[0], sub):
        starts.append(zero if r0 % chunk == 0 else jnp.broadcast_to(G[r0 - 1:r0, :], (sub, width)))
        ends.append(jnp.broadcast_to(G[r0 + sub - 1:r0 + sub, :], (sub, width)))
    return jnp.concatenate(starts, axis=0), jnp.concatenate(ends, axis=0)


def _hgrn_chunk(q, k, v, G, Bs, Be, S, sub, value_major):
    c = q.shape[0]
    nsub = c // sub
    qh = q * jnp.exp(G - Bs)
    kd = k * jnp.exp(Bs - G)
    qg = q * jnp.exp(G)
    row = lax.broadcasted_iota(jnp.int32, (c, c), 0)
    col = lax.broadcasted_iota(jnp.int32, (c, c), 1)
    carried = (lambda: _mm_nt(qg, S)) if value_major else (lambda: _mm(qg, S))
    if nsub == 1:
        att = jnp.where(col <= row, _mm_nt(qh, kd), 0.0)
        o = _mm(att, v) + carried()
    else:
        kh = k * jnp.exp(Be - G)
        lhs = [qh]
        z = jnp.zeros((sub, HEAD_DIM), F32)
        rhs_rows = [jnp.concatenate([kd] + [jnp.zeros_like(kd)] * (nsub - 1), axis=1)]
        for j in range(nsub - 1):
            g_end_j = G[(j + 1) * sub - 1:(j + 1) * sub, :]
            lhs.append(q * jnp.exp(jnp.minimum(G - g_end_j, 0.0)))
            parts = [z] * nsub
            parts[j + 1] = kh[j * sub:(j + 1) * sub, :]
            rhs_rows.append(jnp.concatenate(parts, axis=1))
        n_off = (nsub - 1) * sub
        R = _mm_nt(jnp.concatenate(lhs, axis=1), jnp.concatenate(rhs_rows, axis=0))
        width = c + n_off
        row2 = lax.broadcasted_iota(jnp.int32, (c, width), 0)
        col2 = lax.broadcasted_iota(jnp.int32, (c, width), 1)
        diag_ok = (col2 < c) & (col2 <= row2) & ((col2 // sub) == (row2 // sub))
        off_ok = (col2 >= c) & (col2 < c + n_off) & (((col2 - c) // sub) < (row2 // sub))
        att2 = jnp.where(diag_ok | off_ok, R, 0.0)
        v_rows = [v, v[:n_off, :]]
        if value_major:
            o = _mm(att2, jnp.concatenate(v_rows, axis=0)) + carried()
        else:
            o = _mm(jnp.concatenate([att2, qg], axis=1), jnp.concatenate(v_rows + [S], axis=0))
    g_end = G[c - 1:c, :]
    kk = k * jnp.exp(g_end - G)
    if value_major:
        S_new = jnp.exp(g_end) * S + _mm_tn(v, kk)
    else:
        decay = jnp.broadcast_to(jnp.exp(g_end), (HEAD_DIM, HEAD_DIM)).T
        S_new = decay * S + _mm_tn(kk, v)
    return o, S_new


def _rotary(x, cos, sin):
    return x * cos + pltpu.roll(x, HEAD_DIM // 2, 1) * sin


def _project(x, gain, wqk_hi, wqk_lo, w_rest, cos, sin, lb):
    hn = _rms(x, gain)
    hn_hi, hn_lo = _split2(hn)
    dot = functools.partial(jnp.dot, preferred_element_type=F32)
    qk = (dot(hn_lo, wqk_hi) + dot(hn_hi, wqk_lo)) + dot(hn_hi, wqk_hi)
    rest = dot(hn_hi, w_rest)
    q = jnp.concatenate([_rotary(qk[:, h * HEAD_DIM:(h + 1) * HEAD_DIM], cos, sin) for h in range(N_HEADS)], axis=1)
    k = jnp.concatenate([_rotary(qk[:, MIX_HALF + h * HEAD_DIM:MIX_HALF + (h + 1) * HEAD_DIM], cos, sin)
                         for h in range(N_HEADS)], axis=1)
    v = rest[:, 0:MIX_HALF]
    hq = _silu(rest[:, MIX_HALF:2 * MIX_HALF])
    f = lb + (1.0 - lb) * _sigmoid(rest[:, 2 * MIX_HALF:3 * MIX_HALF])
    hk = 1.0 - f
    g = jnp.log(f)
    hv = rest[:, 3 * MIX_HALF:4 * MIX_HALF]
    hg = rest[:, 4 * MIX_HALF:5 * MIX_HALF]
    return q, k, v, hq, hk, hv, g, hg


def _hgrn_out(o, gain, hg):
    return _rms(o, gain) * _silu(hg)


def _proj_prompt_kernel(x_ref, gain_ref, wqk_hi_ref, wqk_lo_ref, wrest_ref, cos_ref, sin_ref, lb_ref, hgain_ref,
                        q_ref, k_ref, v_ref, kbf_ref, vt_ref, kmean_ref, hout_ref, state_ref, st_ref):
    t = pl.program_id(1)

    @pl.when(t == 0)
    def _():
        st_ref[...] = jnp.zeros_like(st_ref)

    nr, ts, D = x_ref.shape
    cos = jnp.concatenate([cos_ref[...]] * nr, axis=0)
    sin = jnp.concatenate([sin_ref[...]] * nr, axis=0)
    q, k, v, hq, hk, hv, g, hg = _project(x_ref[...].reshape(nr * ts, D), gain_ref[...], wqk_hi_ref[...],
                                          wqk_lo_ref[...], wrest_ref[...], cos, sin, lb_ref[...])
    n_blk = ts // MOBA_BLOCK
    for r in range(nr):
        rows = slice(r * ts, (r + 1) * ts)
        q_ref[r] = q[rows]
        for h in range(N_HEADS):
            k_ref[r, pl.ds(h, ts, stride=N_HEADS), :] = k[rows, h * HEAD_DIM:(h + 1) * HEAD_DIM]
            v_ref[r, pl.ds(h, ts, stride=N_HEADS), :] = v[rows, h * HEAD_DIM:(h + 1) * HEAD_DIM]
        kbf_ref[r] = _bf(k[rows])
        for n in range(n_blk):
            sl = slice(r * ts + n * MOBA_BLOCK, r * ts + (n + 1) * MOBA_BLOCK)
            vt_ref[r, n] = _bf(v[sl, :].T)
            kmean_ref[r, pl.ds(t * n_blk + n, 1), :] = jnp.mean(k[sl, :], axis=0, keepdims=True)

    l_incl = _chunk_matrix(ts, HGRN_CHUNK)
    G = jnp.concatenate([_cumsum_rows(g[r * ts:(r + 1) * ts], l_incl) for r in range(nr)], axis=0)
    Bs, Be = _sub_chunk_bounds(G, HGRN_CHUNK, HGRN_SUB)
    hgain = hgain_ref[...]
    heads = [slice(h * HEAD_DIM, (h + 1) * HEAD_DIM) for h in range(N_HEADS)]
    states = [[st_ref[r, h] for h in range(N_HEADS)] for r in range(nr)]
    outs = [[[] for _ in range(N_HEADS)] for _ in range(nr)]
    for ci in range(ts // HGRN_CHUNK):
        for r in range(nr):
            rs = slice(r * ts + ci * HGRN_CHUNK, r * ts + (ci + 1) * HGRN_CHUNK)
            for h, hs in enumerate(heads):
                o, states[r][h] = _hgrn_chunk(hq[rs, hs], hk[rs, hs], hv[rs, hs], G[rs, hs], Bs[rs, hs], Be[rs, hs],
                                              states[r][h], HGRN_SUB, True)
                outs[r][h].append(o)
    for r in range(nr):
        for h, hs in enumerate(heads):
            st_ref[r, h] = states[r][h]
            hout_ref[r, :, hs] = _bf(_hgrn_out(jnp.concatenate(outs[r][h], axis=0), hgain, hg[r * ts:(r + 1) * ts, hs]))

    @pl.when(t == pl.num_programs(1) - 1)
    def _():
        for r in range(nr):
            for h in range(N_HEADS):
                state_ref[r, h] = st_ref[r, h].T


def _proj_sample_kernel(x_ref, gain_ref, wqk_hi_ref, wqk_lo_ref, wrest_ref, cos_ref, sin_ref, lb_ref, hgain_ref,
                        s0_ref, q_ref, k_ref, v_ref, hout_ref, state_ref, *, seq_len):
    rows = x_ref.shape[0]
    q, k, v, hq, hk, hv, g, hg = _project(x_ref[...], gain_ref[...], wqk_hi_ref[...], wqk_lo_ref[...], wrest_ref[...],
                                          cos_ref[...], sin_ref[...], lb_ref[...])
    q_ref[...] = q
    for h in range(N_HEADS):
        k_ref[pl.ds(h, rows, stride=N_HEADS), :] = k[:, h * HEAD_DIM:(h + 1) * HEAD_DIM]
        v_ref[pl.ds(h, rows, stride=N_HEADS), :] = v[:, h * HEAD_DIM:(h + 1) * HEAD_DIM]
    G = _cumsum_rows(g, _chunk_matrix(rows, seq_len))
    zero = jnp.zeros((seq_len, HEAD_DIM), F32)
    hgain = hgain_ref[...]
    for h in range(N_HEADS):
        hs = slice(h * HEAD_DIM, (h + 1) * HEAD_DIM)
        outs = []
        for si in range(rows // seq_len):
            rs = slice(si * seq_len, (si + 1) * seq_len)
            o, S = _hgrn_chunk(hq[rs, hs], hk[rs, hs], hv[rs, hs], G[rs, hs], zero, zero, s0_ref[si, h], seq_len, False)
            state_ref[si, h] = S
            outs.append(o)
        hout_ref[:, hs] = _bf(_hgrn_out(jnp.concatenate(outs, axis=0), hgain, hg[:, hs]))


def _rotary_tables(pos):
    half = HEAD_DIM // 2
    inv_freq = ROPE_THETA ** (-jnp.arange(half, dtype=F32) / half)
    ang = pos.astype(F32)[:, None] * inv_freq[None, :]
    cos = jnp.cos(ang)
    sin = jnp.sin(ang)
    return jnp.concatenate([cos, cos], axis=1), jnp.concatenate([-sin, sin], axis=1)


def _full(shape):
    return pl.BlockSpec(shape, lambda *_: (0,) * len(shape))


def _proj_prompt(x, gain, wqk_hi, wqk_lo, w_rest, lb, hgain, ts):
    B, S, D = x.shape
    nb = S // MOBA_BLOCK
    cos, sin = _rotary_tables(jnp.arange(S))
    out_shape = (
        jax.ShapeDtypeStruct((B, S, MIX_HALF), F32),
        jax.ShapeDtypeStruct((B, S * N_HEADS, HEAD_DIM), F32),
        jax.ShapeDtypeStruct((B, S * N_HEADS, HEAD_DIM), F32),
        jax.ShapeDtypeStruct((B, S, MIX_HALF), BF16),
        jax.ShapeDtypeStruct((B, nb, MIX_HALF, MOBA_BLOCK), BF16),
        jax.ShapeDtypeStruct((B, nb, MIX_HALF), F32),
        jax.ShapeDtypeStruct((B, S, MIX_HALF), BF16),
        jax.ShapeDtypeStruct((B, N_HEADS, HEAD_DIM, HEAD_DIM), F32),
    )
    nr = 2 if B % 2 == 0 else 1
    tok = pl.BlockSpec((nr, ts, MIX_HALF), lambda b, t: (b, t, 0))
    tok_heads = pl.BlockSpec((nr, ts * N_HEADS, HEAD_DIM), lambda b, t: (b, t, 0))
    return pl.pallas_call(
        _proj_prompt_kernel,
        grid=(B // nr, S // ts),
        in_specs=[
            pl.BlockSpec((nr, ts, D), lambda b, t: (b, t, 0)),
            _full((1, D)), _full(wqk_hi.shape), _full(wqk_lo.shape), _full(w_rest.shape),
            pl.BlockSpec((ts, HEAD_DIM), lambda b, t: (t, 0)),
            pl.BlockSpec((ts, HEAD_DIM), lambda b, t: (t, 0)),
            _full((1, MIX_HALF)), _full((1, HEAD_DIM)),
        ],
        out_specs=(
            tok, tok_heads, tok_heads, tok,
            pl.BlockSpec((nr, ts // MOBA_BLOCK, MIX_HALF, MOBA_BLOCK), lambda b, t: (b, t, 0, 0)),
            pl.BlockSpec((nr, nb, MIX_HALF), lambda b, t: (b, 0, 0)),
            tok,
            pl.BlockSpec((nr, N_HEADS, HEAD_DIM, HEAD_DIM), lambda b, t: (b, 0, 0, 0)),
        ),
        out_shape=out_shape,
        scratch_shapes=[pltpu.VMEM((nr, N_HEADS, HEAD_DIM, HEAD_DIM), F32)],
        compiler_params=pltpu.CompilerParams(dimension_semantics=("arbitrary", "arbitrary"),
                                             vmem_limit_bytes=V7X_VMEM_LIMIT),
        name="proj_prompt",
    )(x, gain, wqk_hi, wqk_lo, w_rest, cos, sin, lb, hgain)


def _proj_sample(x, gain, wqk_hi, wqk_lo, w_rest, lb, hgain, state0, past_len, seqs_per_tile):
    DB, T, D = x.shape
    rows = seqs_per_tile * T
    cos, sin = _rotary_tables(past_len + jnp.arange(T))
    cos = jnp.tile(cos, (seqs_per_tile, 1))
    sin = jnp.tile(sin, (seqs_per_tile, 1))
    out_shape = (
        jax.ShapeDtypeStruct((DB * T, MIX_HALF), F32),
        jax.ShapeDtypeStruct((DB * T * N_HEADS, HEAD_DIM), F32),
        jax.ShapeDtypeStruct((DB * T * N_HEADS, HEAD_DIM), F32),
        jax.ShapeDtypeStruct((DB * T, MIX_HALF), BF16),
        jax.ShapeDtypeStruct((DB, N_HEADS, HEAD_DIM, HEAD_DIM), F32),
    )
    tok = pl.BlockSpec((rows, MIX_HALF), lambda i: (i, 0))
    tok_heads = pl.BlockSpec((rows * N_HEADS, HEAD_DIM), lambda i: (i, 0))
    st = pl.BlockSpec((seqs_per_tile, N_HEADS, HEAD_DIM, HEAD_DIM), lambda i: (i, 0, 0, 0))
    return pl.pallas_call(
        functools.partial(_proj_sample_kernel, seq_len=T),
        grid=(DB // seqs_per_tile,),
        in_specs=[
            pl.BlockSpec((rows, D), lambda i: (i, 0)),
            _full((1, D)), _full(wqk_hi.shape), _full(wqk_lo.shape), _full(w_rest.shape),
            _full((rows, HEAD_DIM)), _full((rows, HEAD_DIM)),
            _full((1, MIX_HALF)), _full((1, HEAD_DIM)),
            st,
        ],
        out_specs=(tok, tok_heads, tok_heads, tok, st),
        out_shape=out_shape,
        compiler_params=pltpu.CompilerParams(dimension_semantics=("arbitrary",),
                                             vmem_limit_bytes=V7X_VMEM_LIMIT),
        name="proj_sample",
    )(x.reshape(DB * T, D), gain, wqk_hi, wqk_lo, w_rest, cos, sin, lb, hgain, state0)


def _top_blocks(gate, n_valid, k_top, axis):
    nb = gate.shape[axis]
    blk = lax.broadcasted_iota(jnp.int32, gate.shape, axis)
    neg = jnp.float32(-jnp.inf)
    g = jnp.where(blk < n_valid, gate, neg)
    sel = jnp.zeros(gate.shape, F32)
    for _ in range(k_top):
        m = jnp.max(g, axis=axis, keepdims=True)
        first = jnp.min(jnp.where(g == m, blk, nb), axis=axis, keepdims=True)
        pick = blk == first
        sel = jnp.where(pick & (m > neg), 1.0, sel)
        g = jnp.where(pick, neg, g)
    return sel


def _moba_prompt_kernel(q_ref, kbf_ref, vt_ref, kmean_ref, o_ref, sel_ref, sa_ref, sb_ref, *, k_top, group, heads):
    i = pl.program_id(2)
    neg = jnp.float32(-jnp.inf)
    start = pl.multiple_of(i * MOBA_BLOCK, MOBA_BLOCK)
    lanes = [slice(hh * HEAD_DIM, (hh + 1) * HEAD_DIM) for hh in range(heads)]
    rows = group * MOBA_BLOCK
    last_group = kbf_ref.shape[1] // rows - 1

    qts, ms, ls, accs = [], [], [], []
    for hh, hl in enumerate(lanes):
        q_t = q_ref[0, :, hl].T
        chosen = _top_blocks(_mm3(kmean_ref[0, :, hl], q_t), i, k_top, 0)
        sel_ref[hh] = jnp.where(chosen > 0.0, 0.0, neg)
        qt = _bf(q_t * (HEAD_DIM ** -0.5 * LOG2_E))
        s = jnp.dot(kbf_ref[0, pl.ds(start, MOBA_BLOCK), hl], qt, preferred_element_type=F32)
        krow = lax.broadcasted_iota(jnp.int32, s.shape, 0)
        qcol = lax.broadcasted_iota(jnp.int32, s.shape, 1)
        s = jnp.where(krow <= qcol, s, neg)
        m = jnp.max(s, axis=0, keepdims=True)
        p = jnp.exp2(s - m)
        qts.append(qt)
        ms.append(m)
        ls.append(jnp.sum(p, axis=0, keepdims=True))
        accs.append(jnp.dot(vt_ref[0, i, hl, :], _bf(p), preferred_element_type=F32))

    def score(g, buf, hh):
        g = jnp.minimum(g, last_group)
        st = pl.multiple_of(g * rows, rows)
        s = jnp.dot(kbf_ref[0, pl.ds(st, rows), lanes[hh]], qts[hh], preferred_element_type=F32)
        mg = None
        for u in range(group):
            su = s[u * MOBA_BLOCK:(u + 1) * MOBA_BLOCK, :] + sel_ref[hh, pl.ds(g * group + u, 1), :]
            buf[hh, u * MOBA_BLOCK:(u + 1) * MOBA_BLOCK, :] = su
            mu = jnp.max(su, axis=0, keepdims=True)
            mg = mu if mg is None else jnp.maximum(mg, mu)
        return mg

    def absorb(g, buf, hh, mg, m, l, acc):
        m_new = jnp.maximum(m, mg)
        alpha = jnp.exp2(m - m_new)
        l = alpha * l
        acc = alpha * acc
        for u in range(group):
            p = jnp.exp2(buf[hh, u * MOBA_BLOCK:(u + 1) * MOBA_BLOCK, :] - m_new)
            l = l + jnp.sum(p, axis=0, keepdims=True)
            acc = acc + jnp.dot(vt_ref[0, g * group + u, lanes[hh], :], _bf(p), preferred_element_type=F32)
        return m_new, l, acc

    def body(k, carry):
        state, m_a = carry
        m_b = [score(2 * k + 1, sb_ref, hh) for hh in range(heads)]
        state = [absorb(2 * k, sa_ref, hh, m_a[hh], *state[hh]) for hh in range(heads)]
        m_a = [score(2 * k + 2, sa_ref, hh) for hh in range(heads)]
        state = [absorb(2 * k + 1, sb_ref, hh, m_b[hh], *state[hh]) for hh in range(heads)]
        return state, m_a

    n_groups = (i + group - 1) // group
    first = [score(0, sa_ref, hh) for hh in range(heads)]
    state, _ = lax.fori_loop(0, (n_groups + 1) // 2, body, ([(ms[hh], ls[hh], accs[hh]) for hh in range(heads)], first))
    for hh, hl in enumerate(lanes):
        _, l, acc = state[hh]
        o_ref[0, :, hl] = _bf((acc / l).T)


def _moba_prompt(q, kbf, vt, kmean):
    B, S, _ = q.shape
    nb = S // MOBA_BLOCK
    k_top = min(MOBA_TOPK, nb - 1)
    group = 2 if nb % 4 == 0 else 1
    assert nb % (2 * group) == 0, "past blocks are absorbed two groups per loop trip"
    heads = 4
    width = heads * HEAD_DIM
    return pl.pallas_call(
        functools.partial(_moba_prompt_kernel, k_top=k_top, group=group, heads=heads),
        grid=(B, N_HEADS // heads, nb),
        in_specs=[
            pl.BlockSpec((1, MOBA_BLOCK, width), lambda b, h, i: (b, i, h)),
            pl.BlockSpec((1, S, width), lambda b, h, i: (b, 0, h)),
            pl.BlockSpec((1, nb, width, MOBA_BLOCK), lambda b, h, i: (b, 0, h, 0)),
            pl.BlockSpec((1, nb, width), lambda b, h, i: (b, 0, h)),
        ],
        out_specs=pl.BlockSpec((1, MOBA_BLOCK, width), lambda b, h, i: (b, i, h)),
        out_shape=jax.ShapeDtypeStruct((B, S, MIX_HALF), BF16),
        scratch_shapes=[pltpu.VMEM((heads, nb, MOBA_BLOCK), F32),
                        pltpu.VMEM((heads, group * MOBA_BLOCK, MOBA_BLOCK), F32),
                        pltpu.VMEM((heads, group * MOBA_BLOCK, MOBA_BLOCK), F32)],
        compiler_params=pltpu.CompilerParams(dimension_semantics=("arbitrary", "arbitrary", "arbitrary"),
                                             vmem_limit_bytes=V7X_VMEM_LIMIT),
        name="moba_prompt",
    )(q, kbf, vt, kmean)


def _heads_to_lanes(ref):
    tokens = ref.shape[0] // N_HEADS
    return jnp.concatenate([ref[pl.ds(h, tokens, stride=N_HEADS), :] for h in range(N_HEADS)], axis=1)


def _moba_sample_scores_kernel(pt_ref, q_ref, kn_ref, vn_ref, expand_ref, *refs, pages_per_step, n_steps, k_top):
    kp_refs = refs[:pages_per_step]
    p_ref, acc0_ref, linv_ref = refs[pages_per_step:pages_per_step + 3]
    qbd_ref, s_ref, ksum_ref = refs[pages_per_step + 3:]
    del pt_ref
    step = pl.program_id(1)
    T = q_ref.shape[1]
    npair = N_HEADS * T
    n_pages = s_ref.shape[0]
    pages_per_blk = MOBA_BLOCK // PAGE_SIZE
    blks_per_step = pages_per_step // pages_per_blk
    neg = jnp.float32(-jnp.inf)

    @pl.when(step == 0)
    def _():
        pair_head = lax.broadcasted_iota(jnp.int32, (npair, MIX_HALF), 0) // T
        lane_head = lax.broadcasted_iota(jnp.int32, (npair, MIX_HALF), 1) // HEAD_DIM
        q4 = jnp.concatenate([q_ref[0]] * N_HEADS, axis=0)
        qbd_ref[...] = jnp.where(pair_head == lane_head, q4, 0.0)

    qs = _bf(qbd_ref[...] * (HEAD_DIM ** -0.5))
    for n in range(blks_per_step):
        ksum = jnp.zeros((1, MIX_HALF), F32)
        for pp in range(pages_per_blk):
            p = n * pages_per_blk + pp
            kp = _heads_to_lanes(kp_refs[p])
            s_ref[step * pages_per_step + p] = _mm_nt(qs, kp)
            ksum = ksum + jnp.sum(kp, axis=0, keepdims=True)
        ksum_ref[pl.ds(step * blks_per_step + n, 1), :] = ksum

    @pl.when(step == n_steps - 1)
    def _():
        n_blocks = ksum_ref.shape[0]
        kmean = ksum_ref[...] * (1.0 / MOBA_BLOCK)
        sel = _top_blocks(_mm3_nt(kmean, qbd_ref[...]), n_blocks, k_top, 0)
        chosen = _mm_tn(sel, expand_ref[...])
        s_own = _mm_nt(qs, _heads_to_lanes(kn_ref))
        key_t = lax.broadcasted_iota(jnp.int32, (npair, T), 1)
        qry_t = lax.broadcasted_iota(jnp.int32, (npair, T), 0) % T
        s_own = jnp.where(key_t <= qry_t, s_own, neg)

        def masked(p):
            return jnp.where(chosen[:, p * PAGE_SIZE:(p + 1) * PAGE_SIZE] > 0.5, s_ref[p], neg)

        m_lanes = masked(0)
        for p in range(1, n_pages):
            m_lanes = jnp.maximum(m_lanes, masked(p))
        m = jnp.maximum(jnp.max(m_lanes, axis=1, keepdims=True), jnp.max(s_own, axis=1, keepdims=True))
        l_lanes = jnp.zeros((npair, PAGE_SIZE), F32)
        for p in range(n_pages):
            w = jnp.exp(masked(p) - m)
            l_lanes = l_lanes + w
            p_ref[0, p] = _bf(w)
        w_own = jnp.exp(s_own - m)
        l = jnp.sum(l_lanes, axis=1, keepdims=True) + jnp.sum(w_own, axis=1, keepdims=True)
        acc0_ref[0] = _mm(w_own, _heads_to_lanes(vn_ref))
        linv_ref[0] = jnp.broadcast_to(1.0 / l, (npair, HEAD_DIM))


def _moba_sample_values_kernel(pt_ref, p_ref, acc0_ref, linv_ref, *refs, pages_per_step, n_steps):
    vp_refs = refs[:pages_per_step]
    o_ref = refs[pages_per_step]
    acc_ref = refs[pages_per_step + 1]
    del pt_ref
    step = pl.program_id(1)
    T = o_ref.shape[1]

    @pl.when(step == 0)
    def _():
        acc_ref[...] = acc0_ref[0]

    acc = acc_ref[...]
    for p in range(pages_per_step):
        vp = _bf(_heads_to_lanes(vp_refs[p]))
        acc = acc + jnp.dot(p_ref[0, step * pages_per_step + p], vp, preferred_element_type=F32)
    acc_ref[...] = acc

    @pl.when(step == n_steps - 1)
    def _():
        linv = linv_ref[0]
        o_ref[0] = jnp.concatenate(
            [acc[h * T:(h + 1) * T, h * HEAD_DIM:(h + 1) * HEAD_DIM] * linv[h * T:(h + 1) * T, :] for h in range(N_HEADS)],
            axis=1)


def _moba_sample(q, k_new, v_new, cache_k, cache_v, page_table, layer, past_len, pages_per_step):
    DB, T, _ = q.shape
    n_pages = page_table.shape[1]
    pages_per_blk = MOBA_BLOCK // PAGE_SIZE
    n_full = past_len // MOBA_BLOCK
    assert n_pages == n_full * pages_per_blk, "the current block must hold no cached rows"
    assert n_pages % pages_per_step == 0 and pages_per_step % pages_per_blk == 0
    n_steps = n_pages // pages_per_step
    k_top = min(MOBA_TOPK, n_full)
    npair = N_HEADS * T
    n_keys = n_pages * PAGE_SIZE
    pt = page_table.astype(jnp.int32).reshape(DB * n_pages)
    expand = (jnp.arange(n_keys)[None, :] // MOBA_BLOCK == jnp.arange(n_full)[:, None]).astype(BF16)

    def page_map(p):
        return lambda b, s, pt_ref: (layer * n_pool + pt_ref[b * n_pages + s * pages_per_step + p], 0)

    per_seq = lambda *tail: (lambda b, s, pt_ref: (b,) + tail)
    page = (PAGE_SIZE * N_HEADS, HEAD_DIM)
    n_pool = cache_k.shape[1]
    cache_k = cache_k.reshape(-1, HEAD_DIM)
    cache_v = cache_v.reshape(-1, HEAD_DIM)
    new_tok = pl.BlockSpec((T * N_HEADS, HEAD_DIM), per_seq(0))
    p_spec = pl.BlockSpec((1, n_pages, npair, PAGE_SIZE), per_seq(0, 0, 0))
    acc_spec = pl.BlockSpec((1, npair, MIX_HALF), per_seq(0, 0))
    linv_spec = pl.BlockSpec((1, npair, HEAD_DIM), per_seq(0, 0))
    params = pltpu.CompilerParams(dimension_semantics=("arbitrary", "arbitrary"), vmem_limit_bytes=V7X_VMEM_LIMIT)

    weights, acc0, linv = pl.pallas_call(
        functools.partial(_moba_sample_scores_kernel, pages_per_step=pages_per_step, n_steps=n_steps, k_top=k_top),
        grid_spec=pltpu.PrefetchScalarGridSpec(
            num_scalar_prefetch=1,
            grid=(DB, n_steps),
            in_specs=[pl.BlockSpec((1, T, MIX_HALF), per_seq(0, 0)), new_tok, new_tok,
                      pl.BlockSpec((n_full, n_keys), lambda b, s, pt_ref: (0, 0))]
            + [pl.BlockSpec(page, page_map(p)) for p in range(pages_per_step)],
            out_specs=(p_spec, acc_spec, linv_spec),
            scratch_shapes=[
                pltpu.VMEM((npair, MIX_HALF), F32),
                pltpu.VMEM((n_pages, npair, PAGE_SIZE), F32),
                pltpu.VMEM((n_full, MIX_HALF), F32),
            ],
        ),
        out_shape=(jax.ShapeDtypeStruct((DB, n_pages, npair, PAGE_SIZE), BF16),
                   jax.ShapeDtypeStruct((DB, npair, MIX_HALF), F32),
                   jax.ShapeDtypeStruct((DB, npair, HEAD_DIM), F32)),
        compiler_params=params,
        name="moba_sample_scores",
    )(pt, q, k_new, v_new, expand, *([cache_k] * pages_per_step))

    return pl.pallas_call(
        functools.partial(_moba_sample_values_kernel, pages_per_step=pages_per_step, n_steps=n_steps),
        grid_spec=pltpu.PrefetchScalarGridSpec(
            num_scalar_prefetch=1,
            grid=(DB, n_steps),
            in_specs=[p_spec, acc_spec, linv_spec] + [pl.BlockSpec(page, page_map(p)) for p in range(pages_per_step)],
            out_specs=pl.BlockSpec((1, T, MIX_HALF), per_seq(0, 0)),
            scratch_shapes=[pltpu.VMEM((npair, MIX_HALF), F32)],
        ),
        out_shape=jax.ShapeDtypeStruct((DB, T, MIX_HALF), F32),
        compiler_params=params,
        name="moba_sample_values",
    )(pt, weights, acc0, linv, *([cache_v] * pages_per_step))


def _memory_kv_kernel(mem_ref, gain_ref, wk_ref, wv_ref, k_ref, v_ref):
    mn = _bf(_rms(mem_ref[0], gain_ref[...]))
    k = jnp.dot(mn, wk_ref[...], preferred_element_type=F32)
    v = jnp.dot(mn, wv_ref[...], preferred_element_type=F32)
    dh = k_ref.shape[-1]
    for h in range(XA_HEADS):
        k_ref[0, :, h, :] = k[:, h * dh:(h + 1) * dh]
        v_ref[0, :, h, :] = v[:, h * dh:(h + 1) * dh]


def _memory_kv(mem, gain, w_ck, w_cv):
    B, M, D = mem.shape
    dh = w_ck.shape[1] // XA_HEADS
    blk = pl.BlockSpec((1, M, D), lambda b: (b, 0, 0))
    out = jax.ShapeDtypeStruct((B, M, XA_HEADS, dh), F32)
    oblk = pl.BlockSpec((1, M, XA_HEADS, dh), lambda b: (b, 0, 0, 0))
    return pl.pallas_call(
        _memory_kv_kernel,
        grid=(B,),
        in_specs=[blk, _full((1, D)), _full(w_ck.shape), _full(w_cv.shape)],
        out_specs=(oblk, oblk),
        out_shape=(out, out),
        compiler_params=pltpu.CompilerParams(dimension_semantics=("arbitrary",), vmem_limit_bytes=V7X_VMEM_LIMIT),
        name="memory_kv",
    )(mem, gain, w_ck, w_cv)


def _mix_kernel(x_ref, att_ref, hout_ref, wo_ref, gpost_ref, gpre_ref, wq_ref, h_ref, q_ref):
    half = att_ref.shape[-1]
    mix = (jnp.dot(_bf(att_ref[...]), wo_ref[0:half, :], preferred_element_type=F32)
           + jnp.dot(hout_ref[...], wo_ref[half:, :], preferred_element_type=F32))
    h = x_ref[...] + _rms(mix, gpost_ref[...])
    h_ref[...] = h
    q = jnp.dot(_bf(_rms(h, gpre_ref[...])), wq_ref[...], preferred_element_type=F32)
    q_ref[...] = _bf(q * (q.shape[-1] // XA_HEADS) ** -0.5)


def _mix(x, att, hout, w_out, g_post, g_pre_x, w_cq, tm):
    N, D = x.shape
    half = att.shape[1]
    return pl.pallas_call(
        _mix_kernel,
        grid=(N // tm,),
        in_specs=[
            pl.BlockSpec((tm, D), lambda i: (i, 0)),
            pl.BlockSpec((tm, half), lambda i: (i, 0)),
            pl.BlockSpec((tm, half), lambda i: (i, 0)),
            _full(w_out.shape), _full((1, D)), _full((1, D)), _full(w_cq.shape),
        ],
        out_specs=(pl.BlockSpec((tm, D), lambda i: (i, 0)), pl.BlockSpec((tm, w_cq.shape[1]), lambda i: (i, 0))),
        out_shape=(jax.ShapeDtypeStruct((N, D), F32), jax.ShapeDtypeStruct((N, w_cq.shape[1]), BF16)),
        compiler_params=pltpu.CompilerParams(dimension_semantics=("arbitrary",), vmem_limit_bytes=V7X_VMEM_LIMIT),
        name="mix_out",
    )(x, att, hout, w_out, g_post, g_pre_x, w_cq)


def _xattn_kernel(q_ref, mk_ref, mv_ref, o_ref):
    groups = q_ref.shape[0]
    dh = q_ref.shape[-1] // XA_HEADS
    pieces = dh // HEAD_DIM
    stride = pieces * XA_HEADS
    M = mk_ref.shape[1] // stride

    def head(ref, gi, h):
        return jnp.concatenate([ref[gi, pl.ds(c * XA_HEADS + h, M, stride=stride), :] for c in range(pieces)], axis=1)

    for gi in range(groups):
        outs = []
        for h in range(XA_HEADS):
            s = _mm_nt(q_ref[gi, :, h * dh:(h + 1) * dh], head(mk_ref, gi, h))
            p = jnp.exp(s - jnp.max(s, axis=-1, keepdims=True))
            l = jnp.sum(p, axis=-1, keepdims=True)
            outs.append(_mm(p, head(mv_ref, gi, h)) / l)
        o_ref[gi] = _bf(jnp.concatenate(outs, axis=1))


def _lane_rows(mem):
    NG, M, H, dh = mem.shape
    pieces = dh // HEAD_DIM
    return mem.reshape(NG, M, H, pieces, HEAD_DIM).transpose(0, 1, 3, 2, 4).reshape(NG, M * pieces * H, HEAD_DIM)


def _xattn(q, mk, mv, groups_per_tile, rows_per_tile):
    NG, R, W = q.shape
    mk = _lane_rows(mk)
    mv = _lane_rows(mv)
    mem_blk = (groups_per_tile,) + mk.shape[1:]
    return pl.pallas_call(
        _xattn_kernel,
        grid=(NG // groups_per_tile, R // rows_per_tile),
        in_specs=[
            pl.BlockSpec((groups_per_tile, rows_per_tile, W), lambda g, r: (g, r, 0)),
            pl.BlockSpec(mem_blk, lambda g, r: (g, 0, 0)),
            pl.BlockSpec(mem_blk, lambda g, r: (g, 0, 0)),
        ],
        out_specs=pl.BlockSpec((groups_per_tile, rows_per_tile, W), lambda g, r: (g, r, 0)),
        out_shape=jax.ShapeDtypeStruct((NG, R, W), BF16),
        compiler_params=pltpu.CompilerParams(dimension_semantics=("arbitrary", "arbitrary"),
                                             vmem_limit_bytes=V7X_VMEM_LIMIT),
        name="cross_attn",
    )(q, mk, mv)


def _ffn_kernel(h_ref, o_ref, wco_ref, gpx_ref, gpf_ref, wup_ref, wdn_ref, gpo_ref, y_ref, *, ff_chunk):
    h = h_ref[...] + _rms(jnp.dot(o_ref[...], wco_ref[...], preferred_element_type=F32), gpx_ref[...])
    hn = _bf(_rms(h, gpf_ref[...]))
    acc = jnp.zeros(h.shape, F32)
    for c in range(wup_ref.shape[1] // ff_chunk):
        cs = slice(c * ff_chunk, (c + 1) * ff_chunk)
        up = jnp.maximum(jnp.dot(hn, wup_ref[:, cs], preferred_element_type=F32), 0.0)
        acc = acc + jnp.dot(_bf(up * up), wdn_ref[cs, :], preferred_element_type=F32)
    y_ref[...] = h + _rms(acc, gpo_ref[...])


def _ffn(h, o, w_co, g_post_x, g_pre_ffn, w_up, w_down, g_post_ffn, tm, ff_chunk):
    N, D = h.shape
    tok = pl.BlockSpec((tm, D), lambda i: (i, 0))
    return pl.pallas_call(
        functools.partial(_ffn_kernel, ff_chunk=ff_chunk),
        grid=(N // tm,),
        in_specs=[tok, pl.BlockSpec((tm, o.shape[1]), lambda i: (i, 0)),
                  _full(w_co.shape), _full((1, D)), _full((1, D)), _full(w_up.shape), _full(w_down.shape),
                  _full((1, D))],
        out_specs=tok,
        out_shape=jax.ShapeDtypeStruct((N, D), F32),
        compiler_params=pltpu.CompilerParams(dimension_semantics=("arbitrary",), vmem_limit_bytes=V7X_VMEM_LIMIT),
        name="co_ffn",
    )(h, o, w_co, g_post_x, g_pre_ffn, w_up, w_down, g_post_ffn)


class _Tiles(NamedTuple):
    proj: int
    tokens: int
    sample_tokens: int
    sample_seqs: int
    xattn_seqs: int
    pages: int
    ff_chunk: int


def _tiles(n_prompt_seq, n_sample_seqs, sample_len, n_pages):
    tokens = min(512, n_prompt_seq)
    return _Tiles(proj=MOBA_BLOCK, tokens=tokens, sample_tokens=min(tokens, n_sample_seqs * sample_len),
                  sample_seqs=min(16, n_sample_seqs), xattn_seqs=min(4, n_sample_seqs), pages=min(64, n_pages),
                  ff_chunk=1024)


def kernel(x_prompt, x_sample, cache_k, cache_v, state_hgrn, cache_mem_k, cache_mem_v, page_table, mem_prompt, norm_pre_mix, w_in, hgrn_lb, hgrn_norm, w_out, norm_post_mix, norm_mem, norm_pre_x, w_cq, w_ck, w_cv, w_co, norm_post_x, norm_pre_ffn, w_up, w_down, norm_post_ffn):
    B, S, D = x_prompt.shape
    DB, T, _ = x_sample.shape
    depth = w_in.shape[0]
    past_len = page_table.shape[1] * PAGE_SIZE
    assert D == 2 * MIX_HALF and S % MOBA_BLOCK == 0 and cache_k.shape[2] == PAGE_SIZE
    tiles = _tiles(S, DB, T, page_table.shape[1])
    xa_w = w_cq.shape[2]

    lb_all = jnp.cumsum(jax.nn.softmax(hgrn_lb.astype(F32), axis=0), axis=0)
    row = lambda a: a.reshape(1, -1)

    yp, ys = x_prompt, x_sample
    kp_l, vp_l, sp_l, mkp_l, mvp_l, ks_l, vs_l, ss_l = [], [], [], [], [], [], [], []
    for l in range(depth):
        w_qk = w_in[l][:, :2 * MIX_HALF]
        wqk_hi = _bf(w_qk)
        wqk_lo = _bf(w_qk - wqk_hi.astype(F32))
        w_rest = _bf(w_in[l][:, 2 * MIX_HALF:])
        lb = row(lb_all[l])
        hgain = row(hgrn_norm[l])
        wo_b, wcq_b, wck_b, wcv_b, wco_b = _bf(w_out[l]), _bf(w_cq[l]), _bf(w_ck[l]), _bf(w_cv[l]), _bf(w_co[l])
        wup_b, wdn_b = _bf(w_up[l]), _bf(w_down[l])

        def trunk_tail(x2d, att2d, hout2d, q_groups, mk, mv, groups_per_tile, rows_per_tile, tile):
            h1, qx = _mix(x2d, att2d, hout2d, wo_b, row(norm_post_mix[l]), row(norm_pre_x[l]), wcq_b, tile)
            ox = _xattn(qx.reshape(q_groups + (xa_w,)), mk, mv, groups_per_tile, rows_per_tile)
            return _ffn(h1, ox.reshape(-1, xa_w), wco_b, row(norm_post_x[l]), row(norm_pre_ffn[l]), wup_b, wdn_b,
                        row(norm_post_ffn[l]), tile, tiles.ff_chunk)

        q, k, v, kbf, vt, kmean, hout, s_fin = _proj_prompt(yp, row(norm_pre_mix[l]), wqk_hi, wqk_lo, w_rest, lb, hgain,
                                                            tiles.proj)
        att = _moba_prompt(q, kbf, vt, kmean)
        mkp, mvp = _memory_kv(mem_prompt, row(norm_mem[l]), wck_b, wcv_b)
        yp = trunk_tail(yp.reshape(B * S, D), att.reshape(B * S, MIX_HALF), hout.reshape(B * S, MIX_HALF),
                        (B, S), mkp, mvp, 1, tiles.tokens, tiles.tokens).reshape(B, S, D)
        kp_l.append(k.reshape(B, S, N_HEADS, HEAD_DIM))
        vp_l.append(v.reshape(B, S, N_HEADS, HEAD_DIM))
        sp_l.append(s_fin)
        mkp_l.append(mkp)
        mvp_l.append(mvp)

        q, k, v, hout, s_new = _proj_sample(ys, row(norm_pre_mix[l]), wqk_hi, wqk_lo, w_rest, lb, hgain,
                                            state_hgrn[l], past_len, tiles.sample_seqs)
        att = _moba_sample(q.reshape(DB, T, MIX_HALF), k, v, cache_k, cache_v, page_table, l, past_len, tiles.pages)
        ys = trunk_tail(ys.reshape(DB * T, D), att.reshape(DB * T, MIX_HALF), hout, (DB, T),
                        cache_mem_k[l], cache_mem_v[l], tiles.xattn_seqs, T, tiles.sample_tokens).reshape(DB, T, D)
        ks_l.append(k.reshape(DB, T, N_HEADS, HEAD_DIM))
        vs_l.append(v.reshape(DB, T, N_HEADS, HEAD_DIM))
        ss_l.append(s_new)
    return (yp, ys, jnp.stack(kp_l), jnp.stack(vp_l), jnp.stack(sp_l), jnp.stack(mkp_l), jnp.stack(mvp_l),
            jnp.stack(ks_l), jnp.stack(vs_l), jnp.stack(ss_l))
```
